```python
import math, functools
import jax, jax.numpy as jnp
from jax import lax
import numpy as np

D_MODEL = 2048
BATCH = 4
SEQ = 2048
DEPTH = 4
DEC_BATCH = 8
DEC_SEQ = 4
PAST_LEN = 16384
PAGE_SIZE = 128

MIX_WIDTH = D_MODEL
SB_HEAD_DIM = 128
SB_WIDTH = MIX_WIDTH // 2
SB_HEADS = SB_WIDTH // SB_HEAD_DIM
GDN_WIDTH = MIX_WIDTH - SB_WIDTH
GDN_DK = 128
GDN_DV = 128
GDN_HEADS = GDN_WIDTH // GDN_DV
GDN_QK = GDN_HEADS * GDN_DK
GDN_QKV = 2 * GDN_QK + GDN_WIDTH
CONV_W = 4
GDN_CHUNK = 64
SB_BLOCK = 128
SB_BIAS_INIT = -8.0
FFN_DIM = ((8 * D_MODEL + 3 * 256 - 1) // (3 * 256)) * 256
IN_COLS = 3 * SB_WIDTH + GDN_QKV + GDN_WIDTH + 2 * GDN_HEADS
SPLIT_IDX = (SB_WIDTH, 2 * SB_WIDTH, 3 * SB_WIDTH, 3 * SB_WIDTH + GDN_QKV,
             3 * SB_WIDTH + GDN_QKV + GDN_WIDTH, 3 * SB_WIDTH + GDN_QKV + GDN_WIDTH + GDN_HEADS)
NORM_EPS = 1e-6

kernel_name = 'stickbreak_gdn_hybrid_step'


def rms_norm(x, w):
    xf = x.astype(jnp.float32)
    y = xf * lax.rsqrt(jnp.mean(xf * xf, axis=-1, keepdims=True) + NORM_EPS)
    return (y * w.astype(jnp.float32)).astype(x.dtype)


def l2_normalize(x):
    return x * lax.rsqrt(jnp.sum(x * x, axis=-1, keepdims=True) + NORM_EPS)


def split_heads(x, n_heads):
    return x.reshape(x.shape[:-1] + (n_heads, x.shape[-1] // n_heads))


def stick_breaking(q, k, v, bias, q_pos, k_pos):
    z = jnp.einsum('bqhd,bkhd->bhqk', q.astype(jnp.float32), k.astype(jnp.float32)) * (SB_HEAD_DIM ** -0.5)
    z = z + bias.astype(jnp.float32)[None, :, None, None]
    mask = k_pos[None, :] < q_pos[:, None]
    log_not = jnp.where(mask, jax.nn.log_sigmoid(-z), 0.0)
    after = lax.cumsum(log_not, axis=3, reverse=True) - log_not
    log_w = jax.nn.log_sigmoid(z) + after
    w = jnp.where(mask, jnp.exp(log_w), 0.0)
    return jnp.einsum('bhqk,bkhd->bqhd', w, v.astype(jnp.float32))


def sb_prompt(q, k, v, bias):
    t = q.shape[1]
    pos = jnp.arange(t)
    outs = []
    for start in range(0, t, SB_BLOCK):
        end = min(start + SB_BLOCK, t)
        outs.append(stick_breaking(q[:, start:end], k[:, :end], v[:, :end], bias, pos[start:end], pos[:end]))
    return jnp.concatenate(outs, axis=1)


def sb_sample(q, k, v, bias, cache_k_l, cache_v_l, page_table):
    b, t = q.shape[0], q.shape[1]
    past_len = page_table.shape[1] * PAGE_SIZE
    k_past = cache_k_l[page_table].reshape(b, past_len, SB_HEADS, SB_HEAD_DIM)
    v_past = cache_v_l[page_table].reshape(b, past_len, SB_HEADS, SB_HEAD_DIM)
    k_all = jnp.concatenate([k_past, k], axis=1)
    v_all = jnp.concatenate([v_past, v], axis=1)
    q_pos = past_len + jnp.arange(t)
    k_pos = jnp.arange(past_len + t)
    return stick_breaking(q, k_all, v_all, bias, q_pos, k_pos)


def chunk_gated_delta(q, k, v, log_g, beta, s0):
    b, t, h, _ = q.shape
    pad = (-t) % GDN_CHUNK
    nc = (t + pad) // GDN_CHUNK

    def to_chunks(x):
        x = jnp.pad(x, [(0, 0), (0, pad)] + [(0, 0)] * (x.ndim - 2))
        x = x.reshape((b, nc, GDN_CHUNK) + x.shape[2:])
        return jnp.moveaxis(x, 3, 2)

    q, k, v, log_g, beta = (to_chunks(a) for a in (q, k, v, log_g, beta))
    q = q * (GDN_DK ** -0.5)
    gam = jnp.cumsum(log_g, axis=-1)
    idx = jnp.arange(GDN_CHUNK)
    incl = idx[:, None] >= idx[None, :]
    strict = idx[:, None] > idx[None, :]
    decay = jnp.exp(jnp.where(incl, gam[..., :, None] - gam[..., None, :], -jnp.inf))
    kb = k * beta[..., None]
    a_mat = jnp.where(strict, jnp.einsum('...id,...jd->...ij', kb, k) * decay, 0.0)
    eye = jnp.eye(GDN_CHUNK, dtype=jnp.float32)
    rhs = jnp.concatenate([v * beta[..., None], kb * jnp.exp(gam)[..., None]], axis=-1)
    sol = lax.linalg.triangular_solve(a_mat + eye, rhs, left_side=True, lower=True)
    u, w = sol[..., :GDN_DV], sol[..., GDN_DV:]
    attn = jnp.einsum('...id,...jd->...ij', q, k) * decay
    q_dec = q * jnp.exp(gam)[..., None]
    k_dec = k * jnp.exp(gam[..., -1:] - gam)[..., None]
    g_tot = jnp.exp(gam[..., -1])
    xs = tuple(jnp.moveaxis(a, 1, 0) for a in (q_dec, k_dec, u, w, attn, g_tot))

    def step(s, inp):
        qd, kd, uc, wc, at, gt = inp
        v_new = uc - jnp.einsum('bhck,bhkv->bhcv', wc, s)
        o = jnp.einsum('bhck,bhkv->bhcv', qd, s) + jnp.einsum('bhij,bhjv->bhiv', at, v_new)
        s = s * gt[..., None, None] + jnp.einsum('bhck,bhcv->bhkv', kd, v_new)
        return s, o

    s_fin, o = lax.scan(step, s0, xs)
    o = jnp.moveaxis(jnp.moveaxis(o, 0, 1), 2, 3).reshape(b, nc * GDN_CHUNK, h, GDN_DV)[:, :t]
    return o, s_fin


def hybrid_layer(h, attend, conv_buf, s0, norm_mix_w, w_in, conv_w, a_log, dt_bias, sb_bias, sb_norm_w,
                 gdn_norm_w, w_out, norm_ffn_w, w_gate, w_up, w_down):
    bsz, t = h.shape[0], h.shape[1]
    xn = rms_norm(h, norm_mix_w)
    proj = xn @ w_in
    qa, ka, va, qkv_b, z, a_in, b_in = jnp.split(proj, SPLIT_IDX, axis=-1)
    qa, ka, va = (split_heads(a, SB_HEADS) for a in (qa, ka, va))
    oa = rms_norm(attend(qa, ka, va, sb_bias), sb_norm_w)
    xc = jnp.concatenate([conv_buf.astype(qkv_b.dtype), qkv_b], axis=1)
    new_buf = xc[:, xc.shape[1] - (CONV_W - 1):]
    y = xc[:, 0:t].astype(jnp.float32) * conv_w[0].astype(jnp.float32)
    for i in range(1, CONV_W):
        y = y + xc[:, i:i + t].astype(jnp.float32) * conv_w[i].astype(jnp.float32)
    y = jax.nn.silu(y)
    qb, kb, vb = jnp.split(y, (GDN_QK, 2 * GDN_QK), axis=-1)
    qb = l2_normalize(split_heads(qb, GDN_HEADS))
    kb = l2_normalize(split_heads(kb, GDN_HEADS))
    vb = split_heads(vb, GDN_HEADS)
    beta = jax.nn.sigmoid(b_in.astype(jnp.float32))
    log_g = -jnp.exp(a_log.astype(jnp.float32)) * jax.nn.softplus(a_in.astype(jnp.float32) + dt_bias.astype(jnp.float32))
    ob, s_new = chunk_gated_delta(qb, kb, vb, log_g, beta, s0.astype(jnp.float32))
    ob = rms_norm(ob, gdn_norm_w) * jax.nn.silu(split_heads(z, GDN_HEADS).astype(jnp.float32))
    mixed = jnp.concatenate([oa.reshape(bsz, t, SB_WIDTH), ob.reshape(bsz, t, GDN_WIDTH)], axis=-1).astype(h.dtype)
    h = h + mixed @ w_out
    xn = rms_norm(h, norm_ffn_w)
    h = h + (jax.nn.silu(xn @ w_gate) * (xn @ w_up)) @ w_down
    return h, ka, va, s_new, new_buf


def setup_inputs(seed: int = 0) -> dict:
    key = jax.random.key(seed)
    ks = jax.random.split(key, 24)
    f32 = jnp.float32
    n_pages = PAST_LEN // PAGE_SIZE
    n_used = DEC_BATCH * n_pages
    n_pool = n_used + max(1, n_used // 4)

    def nrm(k, shape, scale):
        return jax.random.normal(k, shape, f32) * scale

    x_prompt = nrm(ks[0], (BATCH, SEQ, D_MODEL), 1.0)
    x_sample = nrm(ks[1], (DEC_BATCH, DEC_SEQ, D_MODEL), 1.0)
    cache_k = nrm(ks[2], (DEPTH, n_pool, PAGE_SIZE, SB_HEADS, SB_HEAD_DIM), 1.0)
    cache_v = nrm(ks[3], (DEPTH, n_pool, PAGE_SIZE, SB_HEADS, SB_HEAD_DIM), 1.0)
    state_S = nrm(ks[4], (DEPTH, DEC_BATCH, GDN_HEADS, GDN_DK, GDN_DV), GDN_DK ** -0.5)
    state_conv = nrm(ks[5], (DEPTH, DEC_BATCH, CONV_W - 1, GDN_QKV), 1.0)
    page_table = jax.random.permutation(ks[6], n_pool)[:n_used].reshape(DEC_BATCH, n_pages).astype(jnp.int32)
    norm_mix_w = 1.0 + nrm(ks[7], (DEPTH, D_MODEL), 0.02)
    w_in = nrm(ks[8], (DEPTH, D_MODEL, IN_COLS), D_MODEL ** -0.5)
    conv_w = nrm(ks[9], (DEPTH, CONV_W, GDN_QKV), CONV_W ** -0.5)
    a_log = jnp.log(jax.random.uniform(ks[10], (DEPTH, GDN_HEADS), f32, 1.0, 16.0))
    dt = jnp.exp(jax.random.uniform(ks[11], (DEPTH, GDN_HEADS), f32, math.log(1e-3), math.log(1e-1)))
    dt_bias = dt + jnp.log(-jnp.expm1(-dt))
    sb_bias = SB_BIAS_INIT + jax.random.uniform(ks[20], (DEPTH, SB_HEADS), f32, -1.0, 1.0)
    sb_norm_w = 1.0 + nrm(ks[12], (DEPTH, SB_HEADS, SB_HEAD_DIM), 0.02)
    gdn_norm_w = 1.0 + nrm(ks[13], (DEPTH, GDN_DV), 0.02)
    w_out = nrm(ks[14], (DEPTH, MIX_WIDTH, D_MODEL), MIX_WIDTH ** -0.5)
    norm_ffn_w = 1.0 + nrm(ks[15], (DEPTH, D_MODEL), 0.02)
    w_gate = nrm(ks[16], (DEPTH, D_MODEL, FFN_DIM), D_MODEL ** -0.5)
    w_up = nrm(ks[17], (DEPTH, D_MODEL, FFN_DIM), D_MODEL ** -0.5)
    w_down = nrm(ks[18], (DEPTH, FFN_DIM, D_MODEL), FFN_DIM ** -0.5)
    final_norm_w = 1.0 + nrm(ks[19], (D_MODEL,), 0.02)
    return {'x_prompt': x_prompt, 'x_sample': x_sample, 'cache_k': cache_k, 'cache_v': cache_v,
            'state_S': state_S, 'state_conv': state_conv, 'page_table': page_table,
            'norm_mix_w': norm_mix_w, 'w_in': w_in, 'conv_w': conv_w, 'a_log': a_log, 'dt_bias': dt_bias,
            'sb_bias': sb_bias, 'sb_norm_w': sb_norm_w, 'gdn_norm_w': gdn_norm_w, 'w_out': w_out,
            'norm_ffn_w': norm_ffn_w, 'w_gate': w_gate, 'w_up': w_up, 'w_down': w_down,
            'final_norm_w': final_norm_w}


def reference(x_prompt, x_sample, cache_k, cache_v, state_S, state_conv, page_table,
              norm_mix_w, w_in, conv_w, a_log, dt_bias, sb_bias, sb_norm_w, gdn_norm_w, w_out,
              norm_ffn_w, w_gate, w_up, w_down, final_norm_w):
    hp, hs = x_prompt, x_sample
    b_p = x_prompt.shape[0]
    kp, vp, ks_, vs_, sp, ss, cp, cs = [], [], [], [], [], [], [], []
    for l in range(DEPTH):
        params = (norm_mix_w[l], w_in[l], conv_w[l], a_log[l], dt_bias[l], sb_bias[l], sb_norm_w[l],
                  gdn_norm_w[l], w_out[l], norm_ffn_w[l], w_gate[l], w_up[l], w_down[l])
        conv0 = jnp.zeros((b_p, CONV_W - 1, GDN_QKV), x_prompt.dtype)
        s_zero = jnp.zeros((b_p, GDN_HEADS, GDN_DK, GDN_DV), jnp.float32)
        hp, k_new, v_new, s_new, c_new = hybrid_layer(hp, sb_prompt, conv0, s_zero, *params)
        kp.append(k_new); vp.append(v_new); sp.append(s_new); cp.append(c_new)
        attend_s = functools.partial(sb_sample, cache_k_l=cache_k[l], cache_v_l=cache_v[l], page_table=page_table)
        hs, k_new, v_new, s_new, c_new = hybrid_layer(hs, attend_s, state_conv[l], state_S[l], *params)
        ks_.append(k_new); vs_.append(v_new); ss.append(s_new); cs.append(c_new)
    y_prompt = rms_norm(hp, final_norm_w)
    y_sample = rms_norm(hs, final_norm_w)
    return (y_prompt, y_sample, jnp.stack(kp), jnp.stack(vp), jnp.stack(ks_), jnp.stack(vs_),
            jnp.stack(sp), jnp.stack(ss), jnp.stack(cp), jnp.stack(cs))
```

```python
import functools

import jax
import jax.numpy as jnp
from jax import lax
from jax.experimental import pallas as pl
from jax.experimental.pallas import tpu as pltpu

F32 = jnp.float32
BF16 = jnp.bfloat16

NORM_EPS = 1e-6
HEAD_DIM = 128
N_HEADS = 8
GROUP_WIDTH = N_HEADS * HEAD_DIM
CONV_TAPS = 4
GDN_CHUNK = 64
INV_LEVELS = 6
PAGE_SIZE = 128
SB_BLOCK = 256
HALO = 8
N_MAIN_BLOCKS = 7
VMEM_LIMIT_BYTES = 48 * 1024 * 1024


def _cparams(*semantics):
    return pltpu.CompilerParams(dimension_semantics=semantics, vmem_limit_bytes=VMEM_LIMIT_BYTES)


def _dot(a, b):
    return jnp.dot(a, b, preferred_element_type=F32)


def _dot_nt(a, b):
    return lax.dot_general(a, b, (((1,), (1,)), ((), ())), preferred_element_type=F32)


def _dot_tn(a, b):
    return lax.dot_general(a, b, (((0,), (0,)), ((), ())), preferred_element_type=F32)


def _split2(x):
    hi = x.astype(BF16)
    lo = (x - hi.astype(F32)).astype(BF16)
    return hi, lo


def _split3(x):
    hi = x.astype(BF16)
    r = x - hi.astype(F32)
    mid = r.astype(BF16)
    lo = (r - mid.astype(F32)).astype(BF16)
    return hi, mid, lo


def _mm3(a, b):
    ah, al = _split2(a)
    bh, bl = _split2(b)
    return _dot(ah, bh) + (_dot(ah, bl) + _dot(al, bh))


def _sigmoid(x):
    return 1.0 / (1.0 + jnp.exp(-x))


def _softplus(x):
    return jnp.maximum(x, 0.0) + jnp.log1p(jnp.exp(-jnp.abs(x)))


def _rms_scale(x):
    return x * lax.rsqrt(jnp.mean(x * x, axis=-1, keepdims=True) + NORM_EPS)


def _inproj_kernel(x_ref, nw_ref, w_ref, wab_ref, proj_ref, ab_ref, xn_ref):
    @pl.when(pl.program_id(1) == 0)
    def _():
        xn = (_rms_scale(x_ref[...]) * nw_ref[...]).astype(BF16)
        xn_ref[...] = xn
        ab_ref[...] = _dot(xn, wab_ref[...])

    proj_ref[...] = _dot(xn_ref[...], w_ref[...])


def _inproj(x, norm_w, w_main, w_ab, bm):
    m, d = x.shape
    return pl.pallas_call(
        _inproj_kernel,
        out_shape=(jax.ShapeDtypeStruct((m, N_MAIN_BLOCKS * GROUP_WIDTH), F32),
                   jax.ShapeDtypeStruct((m, HEAD_DIM), F32)),
        grid=(m // bm, N_MAIN_BLOCKS),
        in_specs=[pl.BlockSpec((bm, d), lambda i, j: (i, 0)),
                  pl.BlockSpec((1, d), lambda i, j: (0, 0)),
                  pl.BlockSpec((d, GROUP_WIDTH), lambda i, j: (0, j)),
                  pl.BlockSpec((d, HEAD_DIM), lambda i, j: (0, 0))],
        out_specs=(pl.BlockSpec((bm, GROUP_WIDTH), lambda i, j: (i, j)),
                   pl.BlockSpec((bm, HEAD_DIM), lambda i, j: (i, 0))),
        scratch_shapes=[pltpu.VMEM((bm, d), BF16)],
        compiler_params=_cparams("parallel", "arbitrary"),
        name="inproj",
    )(x, norm_w, w_main, w_ab)


def _sb_block(q, kb, vb, u_mat, bias, carry, mask):
    z = _dot_nt(q, kb) + bias
    log_not = -(jnp.maximum(z, 0.0) + jnp.log1p(jnp.exp(-jnp.abs(z))))
    log_beta = log_not + z
    if mask is not None:
        log_not = jnp.where(mask, log_not, 0.0)
    hi, lo = _split2(log_not)
    after = (_dot(hi, u_mat) + _dot(lo, u_mat)) + carry
    w = jnp.exp(log_beta + after)
    if mask is not None:
        w = jnp.where(mask, w, 0.0)
    pv = _dot(w.astype(BF16), vb)
    return pv, carry + jnp.sum(log_not, axis=-1, keepdims=True)


def _sb_prompt_kernel(bias_ref, q_ref, k_ref, v_ref, u_ref, nw_ref, o_ref, *, blk):
    h = pl.program_id(1)
    i = pl.program_id(2)
    bias = bias_ref[h]
    q = (q_ref[...] * HEAD_DIM ** -0.5).astype(BF16)
    u_mat = u_ref[...]
    row = lax.broadcasted_iota(jnp.int32, (blk, blk), 0)
    col = lax.broadcasted_iota(jnp.int32, (blk, blk), 1)

    def kv_block(j):
        start = pl.multiple_of(j * blk, blk)
        return k_ref[pl.ds(start, blk), :].astype(BF16), v_ref[pl.ds(start, blk), :].astype(BF16)

    kb, vb = kv_block(i)
    acc, carry = _sb_block(q, kb, vb, u_mat, bias, jnp.zeros((blk, 1), F32), col < row)

    def body(n, state):
        acc, carry = state
        kb, vb = kv_block(i - 1 - n)
        pv, carry = _sb_block(q, kb, vb, u_mat, bias, carry, None)
        return acc + pv, carry

    acc, _ = lax.fori_loop(0, i, body, (acc, carry))
    o_ref[...] = (_rms_scale(acc) * nw_ref[pl.ds(h, 1), :]).astype(o_ref.dtype)


def _sb_prompt(proj3, sb_bias, u_mat, norm_w):
    b, t, _ = proj3.shape
    blk = SB_BLOCK
    return pl.pallas_call(
        functools.partial(_sb_prompt_kernel, blk=blk),
        out_shape=jax.ShapeDtypeStruct((b, t, GROUP_WIDTH), BF16),
        grid=(b, N_HEADS, t // blk),
        in_specs=[pl.BlockSpec(memory_space=pltpu.SMEM),
                  pl.BlockSpec((None, blk, HEAD_DIM), lambda bi, h, i: (bi, i, h)),
                  pl.BlockSpec((None, t, HEAD_DIM), lambda bi, h, i: (bi, 0, N_HEADS + h)),
                  pl.BlockSpec((None, t, HEAD_DIM), lambda bi, h, i: (bi, 0, 2 * N_HEADS + h)),
                  pl.BlockSpec((blk, blk), lambda bi, h, i: (0, 0)),
                  pl.BlockSpec((N_HEADS, HEAD_DIM), lambda bi, h, i: (0, 0))],
        out_specs=pl.BlockSpec((None, blk, HEAD_DIM), lambda bi, h, i: (bi, i, h)),
        compiler_params=_cparams("parallel", "parallel", "arbitrary"),
        name="sb_prompt",
    )(sb_bias, proj3, proj3, proj3, u_mat, norm_w)


def _sb_sample_kernel(pt_ref, q_ref, bias_ref, knew_ref, vnew_ref, kpg_ref, vpg_ref, u_ref, nw_ref, o_ref,
                      acc_ref, carry_ref, *, n_pages, t_new):
    del pt_ref
    p = pl.program_id(1)
    rows = N_HEADS * t_new
    q = q_ref[...]
    bias = bias_ref[...]
    u_mat = u_ref[...]
    row = lax.broadcasted_iota(jnp.int32, (rows, PAGE_SIZE), 0)
    col = lax.broadcasted_iota(jnp.int32, (rows, PAGE_SIZE), 1)

    @pl.when(p == 0)
    def _():
        pv, carry = _sb_block(q, knew_ref[...].astype(BF16), vnew_ref[...].astype(BF16), u_mat, bias,
                              jnp.zeros((rows, 1), F32), col < lax.rem(row, t_new))
        acc_ref[...] = pv
        carry_ref[...] = jnp.broadcast_to(carry, carry_ref.shape)

    @pl.when(p > 0)
    def _():
        pv, carry = _sb_block(q, kpg_ref[...].astype(BF16), vpg_ref[...].astype(BF16), u_mat, bias,
                              carry_ref[:, 0:1], None)
        acc_ref[...] += pv
        carry_ref[...] = jnp.broadcast_to(carry, carry_ref.shape)

    @pl.when(p == n_pages)
    def _():
        o = jnp.zeros((rows, HEAD_DIM), F32)
        for h in range(N_HEADS):
            in_head = (row >= h * t_new) & (row < (h + 1) * t_new)
            o = o + jnp.where(in_head, acc_ref[:, h * HEAD_DIM:(h + 1) * HEAD_DIM], 0.0)
        o_ref[...] = (_rms_scale(o) * nw_ref[...]).astype(o_ref.dtype)


def _sb_sample(page_table, q_rows, bias_rows, k_new, v_new, cache_k, cache_v, layer, u_mat, norm_rows):
    b, rows, _ = q_rows.shape
    n_pages = page_table.shape[1]
    t_new = rows // N_HEADS

    def page_map(bi, p, pt):
        return (layer, pt[bi, n_pages - 1 - jnp.maximum(p - 1, 0)], 0, 0)

    grid_spec = pltpu.PrefetchScalarGridSpec(
        num_scalar_prefetch=1,
        grid=(b, n_pages + 1),
        in_specs=[pl.BlockSpec((None, rows, GROUP_WIDTH), lambda bi, p, pt: (bi, 0, 0)),
                  pl.BlockSpec((rows, PAGE_SIZE), lambda bi, p, pt: (0, 0)),
                  pl.BlockSpec((None, PAGE_SIZE, GROUP_WIDTH), lambda bi, p, pt: (bi, 0, 0)),
                  pl.BlockSpec((None, PAGE_SIZE, GROUP_WIDTH), lambda bi, p, pt: (bi, 0, 0)),
                  pl.BlockSpec((None, None, PAGE_SIZE, GROUP_WIDTH), page_map),
                  pl.BlockSpec((None, None, PAGE_SIZE, GROUP_WIDTH), page_map),
                  pl.BlockSpec((PAGE_SIZE, PAGE_SIZE), lambda bi, p, pt: (0, 0)),
                  pl.BlockSpec((rows, HEAD_DIM), lambda bi, p, pt: (0, 0))],
        out_specs=pl.BlockSpec((None, rows, HEAD_DIM), lambda bi, p, pt: (bi, 0, 0)),
        scratch_shapes=[pltpu.VMEM((rows, GROUP_WIDTH), F32), pltpu.VMEM((rows, PAGE_SIZE), F32)],
    )
    return pl.pallas_call(
        functools.partial(_sb_sample_kernel, n_pages=n_pages, t_new=t_new),
        out_shape=jax.ShapeDtypeStruct((b, rows, HEAD_DIM), BF16),
        grid_spec=grid_spec,
        compiler_params=_cparams("parallel", "arbitrary"),
        name="sb_sample",
    )(page_table, q_rows, bias_rows, k_new, v_new, cache_k, cache_v, u_mat, norm_rows)


def _gdn_kernel(xq_ref, xk_ref, xv_ref, z_ref, ab_ref, c0q_ref, c0k_ref, c0v_ref, cwq_ref, cwk_ref, cwv_ref,
                alog_ref, dtb_ref, nw_ref, s0_ref, ltri_ref, lvl_ref, ob_ref, sfin_ref,
                s_scr, bq_scr, bk_scr, bv_scr, *, t_valid, n_chunks):
    c = pl.program_id(1)
    ch = GDN_CHUNK

    @pl.when(c == 0)
    def _():
        s_scr[...] = s0_ref[...]
        bq_scr[0:HALO, :] = c0q_ref[...]
        bk_scr[0:HALO, :] = c0k_ref[...]
        bv_scr[0:HALO, :] = c0v_ref[...]

    def conv_silu(x_ref, buf, cw_ref):
        buf[HALO:HALO + ch, :] = x_ref[...]
        first = HALO - (CONV_TAPS - 1)
        y = buf[first:first + ch, :] * cw_ref[0:1, :]
        for tap in range(1, CONV_TAPS):
            y = y + buf[first + tap:first + tap + ch, :] * cw_ref[tap:tap + 1, :]
        buf[0:HALO, :] = buf[ch:ch + HALO, :]
        return y * _sigmoid(y)

    yq = conv_silu(xq_ref, bq_scr, cwq_ref)
    yk = conv_silu(xk_ref, bk_scr, cwk_ref)
    yv = conv_silu(xv_ref, bv_scr, cwv_ref)

    ab = ab_ref[...]
    tok = c * ch + lax.broadcasted_iota(jnp.int32, ab.shape, 0)
    valid = tok < t_valid
    log_g = jnp.where(valid, -jnp.exp(alog_ref[...]) * _softplus(ab + dtb_ref[...]), 0.0)
    beta = jnp.where(valid, _sigmoid(ab), 0.0)
    ltri = ltri_ref[...]
    g_hi, g_mid, g_lo = _split3(log_g)
    gam_c = _dot(ltri, g_hi) + (_dot(ltri, g_mid) + _dot(ltri, g_lo))
    gam_r = gam_c.T
    gam_last = gam_c[ch - 1:ch, :]
    e_gam = jnp.exp(gam_c)
    e_rest = jnp.exp(gam_last - gam_c)
    g_tot = jnp.exp(gam_last)

    ri = lax.broadcasted_iota(jnp.int32, (ch, ch), 0)
    ci = lax.broadcasted_iota(jnp.int32, (ch, ch), 1)
    incl = ri >= ci
    strict = ri > ci
    eye = jnp.where(ri == ci, 1.0, 0.0).astype(F32)

    for h in range(N_HEADS):
        sl = slice(h * HEAD_DIM, (h + 1) * HEAD_DIM)
        q = yq[:, sl]
        k = yk[:, sl]
        v = yv[:, sl]
        q = q * lax.rsqrt(jnp.sum(q * q, axis=-1, keepdims=True) + NORM_EPS) * HEAD_DIM ** -0.5
        k = k * lax.rsqrt(jnp.sum(k * k, axis=-1, keepdims=True) + NORM_EPS)
        beta_h = beta[:, N_HEADS + h:N_HEADS + h + 1]
        e_gam_h = e_gam[:, h:h + 1]
        diff = gam_c[:, h:h + 1] - gam_r[h:h + 1, :]
        decay = jnp.where(incl, jnp.exp(jnp.where(incl, diff, 0.0)), 0.0)
        k_beta = k * beta_h
        k_bf = k.astype(BF16)
        a_mat = jnp.where(strict, _dot_nt(k_beta.astype(BF16), k_bf) * decay, 0.0)
        attn = _dot_nt(q.astype(BF16), k_bf) * decay
        inv = eye - a_mat * lvl_ref[0]
        for lvl in range(1, INV_LEVELS):
            inv = inv - _mm3(inv, _mm3(a_mat * lvl_ref[lvl], inv))
        u = _mm3(inv, v * beta_h)
        w = _mm3(inv, k_beta * e_gam_h)
        s = s_scr[h]
        s_bf = s.astype(BF16)
        v_new = u - _dot(w.astype(BF16), s_bf)
        v_new_bf = v_new.astype(BF16)
        o = _dot((q * e_gam_h).astype(BF16), s_bf) + _dot(attn.astype(BF16), v_new_bf)
        k_rest = (k * e_rest[:, h:h + 1]).astype(BF16)
        s_scr[h] = s * g_tot[:, h:h + 1] + _dot_tn(k_rest, v_new_bf)
        z = z_ref[:, sl]
        ob_ref[:, sl] = (_rms_scale(o) * nw_ref[...] * (z * _sigmoid(z))).astype(ob_ref.dtype)

    @pl.when(c == n_chunks - 1)
    def _():
        sfin_ref[...] = s_scr[...]


def _gdn(proj3, ab3, conv0, conv_w, alog_row, dtb_row, norm_w, s0, ltri, lvl_masks, t_valid):
    b, t, _ = proj3.shape
    ch = GDN_CHUNK
    n_chunks = t // ch
    gw = GROUP_WIDTH

    def col_spec(rows, blk_col):
        return pl.BlockSpec((None, rows, gw), lambda bi, c: (bi, c if rows == ch else 0, blk_col))

    def cw_spec(blk_col):
        return pl.BlockSpec((HALO, gw), lambda bi, c: (0, blk_col))

    row_spec = pl.BlockSpec((1, HEAD_DIM), lambda bi, c: (0, 0))
    state_spec = pl.BlockSpec((None, N_HEADS, HEAD_DIM, HEAD_DIM), lambda bi, c: (bi, 0, 0, 0))
    return pl.pallas_call(
        functools.partial(_gdn_kernel, t_valid=t_valid, n_chunks=n_chunks),
        out_shape=(jax.ShapeDtypeStruct((b, t, gw), BF16),
                   jax.ShapeDtypeStruct((b, N_HEADS, HEAD_DIM, HEAD_DIM), F32)),
        grid=(b, n_chunks),
        in_specs=[col_spec(ch, 3), col_spec(ch, 4), col_spec(ch, 5), col_spec(ch, 6),
                  pl.BlockSpec((None, ch, HEAD_DIM), lambda bi, c: (bi, c, 0)),
                  col_spec(HALO, 0), col_spec(HALO, 1), col_spec(HALO, 2),
                  cw_spec(0), cw_spec(1), cw_spec(2),
                  row_spec, row_spec, row_spec, state_spec,
                  pl.BlockSpec((ch, ch), lambda bi, c: (0, 0)),
                  pl.BlockSpec((INV_LEVELS, ch, ch), lambda bi, c: (0, 0, 0))],
        out_specs=(pl.BlockSpec((None, ch, gw), lambda bi, c: (bi, c, 0)), state_spec),
        scratch_shapes=[pltpu.VMEM((N_HEADS, HEAD_DIM, HEAD_DIM), F32),
                        pltpu.VMEM((HALO + ch, gw), F32),
                        pltpu.VMEM((HALO + ch, gw), F32),
                        pltpu.VMEM((HALO + ch, gw), F32)],
        compiler_params=_cparams("parallel", "arbitrary"),
        name="gdn",
    )(proj3, proj3, proj3, proj3, ab3, conv0, conv0, conv0, conv_w, conv_w, conv_w,
      alog_row, dtb_row, norm_w, s0, ltri, lvl_masks)


def _outproj_kernel(h_ref, oa_ref, ob_ref, wa_ref, wb_ref, o_ref):
    o_ref[...] = h_ref[...] + (_dot(oa_ref[...], wa_ref[...]) + _dot(ob_ref[...], wb_ref[...]))


def _outproj(h, oa, ob, w_out, bm):
    m, d = h.shape
    gw = GROUP_WIDTH
    return pl.pallas_call(
        _outproj_kernel,
        out_shape=jax.ShapeDtypeStruct((m, d), F32),
        grid=(m // bm,),
        in_specs=[pl.BlockSpec((bm, d), lambda i: (i, 0)),
                  pl.BlockSpec((bm, gw), lambda i: (i, 0)),
                  pl.BlockSpec((bm, gw), lambda i: (i, 0)),
                  pl.BlockSpec((gw, d), lambda i: (0, 0)),
                  pl.BlockSpec((gw, d), lambda i: (1, 0))],
        out_specs=pl.BlockSpec((bm, d), lambda i: (i, 0)),
        compiler_params=_cparams("parallel"),
        name="outproj",
    )(h, oa, ob, w_out, w_out)


def _ffn_kernel(h_ref, nw_ref, wg_ref, wu_ref, wd_ref, fnw_ref, o_ref, xn_ref, acc_ref, *, n_f, final_norm):
    f = pl.program_id(1)

    @pl.when(f == 0)
    def _():
        xn_ref[...] = (_rms_scale(h_ref[...]) * nw_ref[...]).astype(BF16)
        acc_ref[...] = jnp.zeros_like(acc_ref)

    xn = xn_ref[...]
    g = _dot(xn, wg_ref[...])
    act = (g * _sigmoid(g)) * _dot(xn, wu_ref[...])
    acc_ref[...] += _dot(act.astype(BF16), wd_ref[...])

    @pl.when(f == n_f - 1)
    def _():
        out = h_ref[...] + acc_ref[...]
        if final_norm:
            out = _rms_scale(out) * fnw_ref[...]
        o_ref[...] = out


def _ffn(h, norm_w, w_gate, w_up, w_down, final_w, bm, bf, final_norm):
    m, d = h.shape
    f_dim = w_gate.shape[1]
    n_f = f_dim // bf
    return pl.pallas_call(
        functools.partial(_ffn_kernel, n_f=n_f, final_norm=final_norm),
        out_shape=jax.ShapeDtypeStruct((m, d), F32),
        grid=(m // bm, n_f),
        in_specs=[pl.BlockSpec((bm, d), lambda i, f: (i, 0)),
                  pl.BlockSpec((1, d), lambda i, f: (0, 0)),
                  pl.BlockSpec((d, bf), lambda i, f: (0, f)),
                  pl.BlockSpec((d, bf), lambda i, f: (0, f)),
                  pl.BlockSpec((bf, d), lambda i, f: (f, 0)),
                  pl.BlockSpec((1, d), lambda i, f: (0, 0))],
        out_specs=pl.BlockSpec((bm, d), lambda i, f: (i, 0)),
        scratch_shapes=[pltpu.VMEM((bm, d), BF16), pltpu.VMEM((bm, d), F32)],
        compiler_params=_cparams("parallel", "arbitrary"),
        name="ffn",
    )(h, norm_w, w_gate, w_up, w_down, final_w)


def _row_block(m, target):
    return target if m % target == 0 else m


def _excl_upper(n):
    j = lax.broadcasted_iota(jnp.int32, (n, n), 0)
    s = lax.broadcasted_iota(jnp.int32, (n, n), 1)
    return (j > s).astype(BF16)


def _pad_lanes(row, width):
    return jnp.pad(row, (0, width - row.shape[0])).reshape(1, width)


def kernel(x_prompt, x_sample, cache_k, cache_v, state_S, state_conv, page_table, norm_mix_w, w_in, conv_w, a_log, dt_bias, sb_bias, sb_norm_w, gdn_norm_w, w_out, norm_ffn_w, w_gate, w_up, w_down, final_norm_w):
    bp, tp, d = x_prompt.shape
    bs, ts, _ = x_sample.shape
    depth = w_in.shape[0]
    gw = GROUP_WIDTH
    main_cols = N_MAIN_BLOCKS * gw
    qkv_cols = 3 * gw
    ch = GDN_CHUNK
    assert w_in.shape[2] == main_cols + 2 * N_HEADS and tp % SB_BLOCK == 0 and ts <= ch
    assert cache_k.shape[2:] == (PAGE_SIZE, N_HEADS, HEAD_DIM)

    w_in_bf = w_in.astype(BF16)
    w_ab_bf = jnp.pad(w_in_bf[:, :, main_cols:], ((0, 0), (0, 0), (0, HEAD_DIM - 2 * N_HEADS)))
    w_out_bf = w_out.astype(BF16)
    w_gate_bf = w_gate.astype(BF16)
    w_up_bf = w_up.astype(BF16)
    w_down_bf = w_down.astype(BF16)
    conv_w8 = jnp.pad(conv_w, ((0, 0), (0, HALO - CONV_TAPS), (0, 0)))
    cache_k4 = cache_k.reshape(cache_k.shape[:3] + (gw,))
    cache_v4 = cache_v.reshape(cache_v.shape[:3] + (gw,))

    u_prompt = _excl_upper(SB_BLOCK)
    u_page = _excl_upper(PAGE_SIZE)
    ri = lax.broadcasted_iota(jnp.int32, (ch, ch), 0)
    ci = lax.broadcasted_iota(jnp.int32, (ch, ch), 1)
    ltri = (ri >= ci).astype(BF16)
    lvl_masks = jnp.stack([((ri // (2 * b) == ci // (2 * b)) & ((ri & b) != 0) & ((ci & b) == 0)).astype(F32)
                           for b in (1 << e for e in range(INV_LEVELS))])
    head_eye = jnp.eye(N_HEADS, dtype=F32)
    final_w = final_norm_w.reshape(1, d)

    hp = x_prompt.reshape(bp * tp, d)
    hs = x_sample.reshape(bs * ts, d)
    bm_p = _row_block(bp * tp, 512)
    bm_s = bs * ts
    conv0_p = jnp.zeros((bp, HALO, qkv_cols), F32)
    s0_p = jnp.zeros((bp, N_HEADS, HEAD_DIM, HEAD_DIM), F32)

    outs = [[] for _ in range(8)]
    for l in range(depth):
        nmw = norm_mix_w[l].reshape(1, d)
        nfw = norm_ffn_w[l].reshape(1, d)
        alog_row = _pad_lanes(a_log[l], HEAD_DIM)
        dtb_row = _pad_lanes(dt_bias[l], HEAD_DIM)
        gnw = gdn_norm_w[l].reshape(1, HEAD_DIM)
        last = l == depth - 1

        proj, ab = _inproj(hp, nmw, w_in_bf[l], w_ab_bf[l], bm_p)
        proj3 = proj.reshape(bp, tp, main_cols)
        oa = _sb_prompt(proj3, sb_bias[l], u_prompt, sb_norm_w[l])
        ob, s_new = _gdn(proj3, ab.reshape(bp, tp, HEAD_DIM), conv0_p, conv_w8[l], alog_row, dtb_row, gnw,
                         s0_p, ltri, lvl_masks, tp)
        hp = _outproj(hp, oa.reshape(bp * tp, gw), ob.reshape(bp * tp, gw), w_out_bf[l], bm_p)
        hp = _ffn(hp, nfw, w_gate_bf[l], w_up_bf[l], w_down_bf[l], final_w, bm_p, 512, last)
        outs[0].append(proj3[:, :, gw:2 * gw].reshape(bp, tp, N_HEADS, HEAD_DIM))
        outs[1].append(proj3[:, :, 2 * gw:3 * gw].reshape(bp, tp, N_HEADS, HEAD_DIM))
        outs[4].append(s_new)
        outs[6].append(proj3[:, tp - (CONV_TAPS - 1):, 3 * gw:6 * gw])

        proj, ab = _inproj(hs, nmw, w_in_bf[l], w_ab_bf[l], bm_s)
        proj3 = proj.reshape(bs, ts, main_cols)
        k_new = proj3[:, :, gw:2 * gw]
        v_new = proj3[:, :, 2 * gw:3 * gw]
        q4 = proj3[:, :, :gw].reshape(bs, ts, N_HEADS, HEAD_DIM) * HEAD_DIM ** -0.5
        q_rows = jnp.einsum('bthd,hg->bhtgd', q4, head_eye).reshape(bs, N_HEADS * ts, gw).astype(BF16)
        bias_rows = jnp.broadcast_to(jnp.repeat(sb_bias[l], ts)[:, None], (N_HEADS * ts, PAGE_SIZE))
        norm_rows = jnp.repeat(sb_norm_w[l], ts, axis=0)
        pad_keys = ((0, 0), (0, PAGE_SIZE - ts), (0, 0))
        oa = _sb_sample(page_table, q_rows, bias_rows, jnp.pad(k_new, pad_keys), jnp.pad(v_new, pad_keys),
                        cache_k4, cache_v4, l, u_page, norm_rows)
        oa = oa.reshape(bs, N_HEADS, ts, HEAD_DIM).transpose(0, 2, 1, 3).reshape(bs * ts, gw)
        pad_chunk = ((0, 0), (0, ch - ts), (0, 0))
        conv0_s = jnp.pad(state_conv[l], ((0, 0), (HALO - (CONV_TAPS - 1), 0), (0, 0)))
        ob, s_new = _gdn(jnp.pad(proj3, pad_chunk), jnp.pad(ab.reshape(bs, ts, HEAD_DIM), pad_chunk), conv0_s,
                         conv_w8[l], alog_row, dtb_row, gnw, state_S[l], ltri, lvl_masks, ts)
        hs = _outproj(hs, oa, ob[:, :ts].reshape(bs * ts, gw), w_out_bf[l], bm_s)
        hs = _ffn(hs, nfw, w_gate_bf[l], w_up_bf[l], w_down_bf[l], final_w, bm_s, 512, last)
        outs[2].append(k_new.reshape(bs, ts, N_HEADS, HEAD_DIM))
        outs[3].append(v_new.reshape(bs, ts, N_HEADS, HEAD_DIM))
        outs[5].append(s_new)
        xc_tail = jnp.concatenate([state_conv[l], proj3[:, :, 3 * gw:6 * gw]], axis=1)
        outs[7].append(xc_tail[:, xc_tail.shape[1] - (CONV_TAPS - 1):])

    return (hp.reshape(bp, tp, d), hs.reshape(bs, ts, d)) + tuple(jnp.stack(o) for o in outs)
```

```python
import functools

import jax
import jax.numpy as jnp
from jax import lax
from jax.experimental import pallas as pl
from jax.experimental.pallas import tpu as pltpu

F32 = jnp.float32
BF16 = jnp.bfloat16

NORM_EPS = 1e-6
HEAD_DIM = 128
N_HEADS = 8
GROUP_WIDTH = N_HEADS * HEAD_DIM
CONV_TAPS = 4
GDN_CHUNK = 64
INV_LEVELS = 6
PAGE_SIZE = 128
SB_BLOCK = 256
SAMPLE_PAGES_PER_STEP = 8
QUAD = 4
HALO = 8
N_MAIN_BLOCKS = 7
VMEM_LIMIT_BYTES = 48 * 1024 * 1024


def _cparams(*semantics):
    return pltpu.CompilerParams(dimension_semantics=semantics, vmem_limit_bytes=VMEM_LIMIT_BYTES)


def _dot(a, b):
    return jnp.dot(a, b, preferred_element_type=F32)


def _dot_nt(a, b):
    return lax.dot_general(a, b, (((1,), (1,)), ((), ())), preferred_element_type=F32)


def _dot_tn(a, b):
    return lax.dot_general(a, b, (((0,), (0,)), ((), ())), preferred_element_type=F32)


def _split2(x):
    hi = x.astype(BF16)
    lo = (x - hi.astype(F32)).astype(BF16)
    return hi, lo


def _split3(x):
    hi = x.astype(BF16)
    r = x - hi.astype(F32)
    mid = r.astype(BF16)
    lo = (r - mid.astype(F32)).astype(BF16)
    return hi, mid, lo


def _lane_block_diag(x, n):
    w = x.shape[1] // n
    lane = lax.broadcasted_iota(jnp.int32, x.shape, 1)
    return jnp.concatenate([jnp.where((lane >= j * w) & (lane < (j + 1) * w), x, 0.0) for j in range(n)], axis=0)


def _mm3_block_diag(x, y, n):
    yh, yl = _split2(_lane_block_diag(y, n))
    xh, xl = _split2(x)
    m = x.shape[0]
    both = _dot(jnp.concatenate([xh, xl], axis=0), yh)
    return both[:m] + (_dot(xh, yl) + both[m:])


def _sigmoid(x):
    return 1.0 / (1.0 + jnp.exp(-x))


def _softplus(x):
    return jnp.maximum(x, 0.0) + jnp.log1p(jnp.exp(-jnp.abs(x)))


def _rms_scale(x):
    return x * lax.rsqrt(jnp.mean(x * x, axis=-1, keepdims=True) + NORM_EPS)


def _inproj_kernel(x_ref, nw_ref, w_ref, wab_ref, proj_ref, ab_ref, xn_ref):
    @pl.when(pl.program_id(1) == 0)
    def _():
        xn = (_rms_scale(x_ref[...]) * nw_ref[...]).astype(BF16)
        xn_ref[...] = xn
        ab_ref[...] = _dot(xn, wab_ref[...])

    proj_ref[...] = _dot(xn_ref[...], w_ref[...])


def _inproj(x, norm_w, w_main, w_ab, bm):
    m, d = x.shape
    return pl.pallas_call(
        _inproj_kernel,
        out_shape=(jax.ShapeDtypeStruct((m, N_MAIN_BLOCKS * GROUP_WIDTH), F32),
                   jax.ShapeDtypeStruct((m, HEAD_DIM), F32)),
        grid=(m // bm, N_MAIN_BLOCKS),
        in_specs=[pl.BlockSpec((bm, d), lambda i, j: (i, 0)),
                  pl.BlockSpec((1, d), lambda i, j: (0, 0)),
                  pl.BlockSpec((d, GROUP_WIDTH), lambda i, j: (0, j)),
                  pl.BlockSpec((d, HEAD_DIM), lambda i, j: (0, 0))],
        out_specs=(pl.BlockSpec((bm, GROUP_WIDTH), lambda i, j: (i, j)),
                   pl.BlockSpec((bm, HEAD_DIM), lambda i, j: (i, 0))),
        scratch_shapes=[pltpu.VMEM((bm, d), BF16)],
        compiler_params=_cparams("parallel", "arbitrary"),
        name="inproj",
    )(x, norm_w, w_main, w_ab)


def _sb_block(q, kb, vb, u_mat, bias, carry, mask):
    z = _dot_nt(q, kb) + bias
    log_not = -(jnp.maximum(z, 0.0) + jnp.log1p(jnp.exp(-jnp.abs(z))))
    log_beta = log_not + z
    if mask is not None:
        log_not = jnp.where(mask, log_not, 0.0)
    hi, lo = _split2(log_not)
    after = (_dot(hi, u_mat) + _dot(lo, u_mat)) + carry
    w = jnp.exp(log_beta + after)
    if mask is not None:
        w = jnp.where(mask, w, 0.0)
    pv = _dot(w.astype(BF16), vb)
    return pv, carry + jnp.sum(log_not, axis=-1, keepdims=True)


def _sb_prompt_kernel(bias_ref, q_ref, k_ref, v_ref, u_ref, nw_ref, o_ref, *, blk):
    h = pl.program_id(1)
    i = pl.program_id(2)
    bias = bias_ref[h]
    q0 = (q_ref[0:blk, :] * HEAD_DIM ** -0.5).astype(BF16)
    q1 = (q_ref[blk:2 * blk, :] * HEAD_DIM ** -0.5).astype(BF16)
    u_mat = u_ref[...]
    row = lax.broadcasted_iota(jnp.int32, (blk, blk), 0)
    col = lax.broadcasted_iota(jnp.int32, (blk, blk), 1)

    def kv_block(j):
        start = pl.multiple_of(j * blk, blk)
        return k_ref[pl.ds(start, blk), :].astype(BF16), v_ref[pl.ds(start, blk), :].astype(BF16)

    mask = col < row
    zero = jnp.zeros((blk, 1), F32)
    kb, vb = kv_block(2 * i + 1)
    acc1, carry1 = _sb_block(q1, kb, vb, u_mat, bias, zero, mask)
    kb, vb = kv_block(2 * i)
    acc0, carry0 = _sb_block(q0, kb, vb, u_mat, bias, zero, mask)
    pv, carry1 = _sb_block(q1, kb, vb, u_mat, bias, carry1, None)
    acc1 = acc1 + pv

    def body(n, state):
        acc0, carry0, acc1, carry1 = state
        kb, vb = kv_block(2 * i - 1 - n)
        pv0, carry0 = _sb_block(q0, kb, vb, u_mat, bias, carry0, None)
        pv1, carry1 = _sb_block(q1, kb, vb, u_mat, bias, carry1, None)
        return acc0 + pv0, carry0, acc1 + pv1, carry1

    acc0, _, acc1, _ = lax.fori_loop(0, 2 * i, body, (acc0, carry0, acc1, carry1))
    norm_w = nw_ref[pl.ds(h, 1), :]
    o_ref[0:blk, :] = (_rms_scale(acc0) * norm_w).astype(o_ref.dtype)
    o_ref[blk:2 * blk, :] = (_rms_scale(acc1) * norm_w).astype(o_ref.dtype)


def _sb_prompt(proj3, sb_bias, u_mat, norm_w):
    b, t, _ = proj3.shape
    blk = SB_BLOCK
    return pl.pallas_call(
        functools.partial(_sb_prompt_kernel, blk=blk),
        out_shape=jax.ShapeDtypeStruct((b, t, GROUP_WIDTH), BF16),
        grid=(b, N_HEADS, t // (2 * blk)),
        in_specs=[pl.BlockSpec(memory_space=pltpu.SMEM),
                  pl.BlockSpec((None, 2 * blk, HEAD_DIM), lambda bi, h, i: (bi, i, h)),
                  pl.BlockSpec((None, t, HEAD_DIM), lambda bi, h, i: (bi, 0, N_HEADS + h)),
                  pl.BlockSpec((None, t, HEAD_DIM), lambda bi, h, i: (bi, 0, 2 * N_HEADS + h)),
                  pl.BlockSpec((blk, blk), lambda bi, h, i: (0, 0)),
                  pl.BlockSpec((N_HEADS, HEAD_DIM), lambda bi, h, i: (0, 0))],
        out_specs=pl.BlockSpec((None, 2 * blk, HEAD_DIM), lambda bi, h, i: (bi, i, h)),
        compiler_params=_cparams("parallel", "parallel", "arbitrary"),
        name="sb_prompt",
    )(sb_bias, proj3, proj3, proj3, u_mat, norm_w)


def _page_rows(page_ref):
    return jnp.concatenate([page_ref[pl.ds(h, PAGE_SIZE, stride=N_HEADS), :] for h in range(N_HEADS)],
                           axis=1).astype(BF16)


def _sb_sample_kernel(pt_ref, q_ref, bias_ref, knew_ref, vnew_ref, *rest, n_steps, t_new, group):
    del pt_ref
    k_refs, v_refs = rest[:group], rest[group:2 * group]
    u_ref, nw_ref, o_ref, acc_ref, carry_ref = rest[2 * group:]
    p = pl.program_id(1)
    rows = N_HEADS * t_new
    q = q_ref[...]
    bias = bias_ref[...]
    u_mat = u_ref[...]
    row = lax.broadcasted_iota(jnp.int32, (rows, PAGE_SIZE), 0)
    col = lax.broadcasted_iota(jnp.int32, (rows, PAGE_SIZE), 1)

    @pl.when(p == 0)
    def _():
        pv, carry = _sb_block(q, knew_ref[...].astype(BF16), vnew_ref[...].astype(BF16), u_mat, bias,
                              jnp.zeros((rows, 1), F32), col < lax.rem(row, t_new))
        acc_ref[...] = pv
        carry_ref[...] = jnp.broadcast_to(carry, carry_ref.shape)

    @pl.when(p > 0)
    def _():
        carry = carry_ref[:, 0:1]
        acc = acc_ref[...]
        for g in range(group):
            pv, carry = _sb_block(q, _page_rows(k_refs[g]), _page_rows(v_refs[g]), u_mat, bias, carry, None)
            acc = acc + pv
        acc_ref[...] = acc
        carry_ref[...] = jnp.broadcast_to(carry, carry_ref.shape)

    @pl.when(p == n_steps)
    def _():
        o = jnp.zeros((rows, HEAD_DIM), F32)
        for h in range(N_HEADS):
            in_head = (row >= h * t_new) & (row < (h + 1) * t_new)
            o = o + jnp.where(in_head, acc_ref[:, h * HEAD_DIM:(h + 1) * HEAD_DIM], 0.0)
        o_ref[...] = (_rms_scale(o) * nw_ref[...]).astype(o_ref.dtype)


def _sb_sample(page_table, q_rows, bias_rows, k_new, v_new, cache_k, cache_v, layer, u_mat, norm_rows):
    b, rows, _ = q_rows.shape
    n_pages = page_table.shape[1]
    t_new = rows // N_HEADS
    group = max(g for g in range(1, SAMPLE_PAGES_PER_STEP + 1) if n_pages % g == 0)
    n_steps = n_pages // group

    def page_spec(g):
        def page_map(bi, p, pt):
            return (layer, pt[bi, n_pages - 1 - (jnp.maximum(p - 1, 0) * group + g)], 0, 0)
        return pl.BlockSpec((None, None, PAGE_SIZE * N_HEADS, HEAD_DIM), page_map)

    page_specs = [page_spec(g) for g in range(group)]
    grid_spec = pltpu.PrefetchScalarGridSpec(
        num_scalar_prefetch=1,
        grid=(b, n_steps + 1),
        in_specs=[pl.BlockSpec((None, rows, GROUP_WIDTH), lambda bi, p, pt: (bi, 0, 0)),
                  pl.BlockSpec((rows, PAGE_SIZE), lambda bi, p, pt: (0, 0)),
                  pl.BlockSpec((None, PAGE_SIZE, GROUP_WIDTH), lambda bi, p, pt: (bi, 0, 0)),
                  pl.BlockSpec((None, PAGE_SIZE, GROUP_WIDTH), lambda bi, p, pt: (bi, 0, 0))]
                 + page_specs + page_specs
                 + [pl.BlockSpec((PAGE_SIZE, PAGE_SIZE), lambda bi, p, pt: (0, 0)),
                    pl.BlockSpec((rows, HEAD_DIM), lambda bi, p, pt: (0, 0))],
        out_specs=pl.BlockSpec((None, rows, HEAD_DIM), lambda bi, p, pt: (bi, 0, 0)),
        scratch_shapes=[pltpu.VMEM((rows, GROUP_WIDTH), F32), pltpu.VMEM((rows, PAGE_SIZE), F32)],
    )
    return pl.pallas_call(
        functools.partial(_sb_sample_kernel, n_steps=n_steps, t_new=t_new, group=group),
        out_shape=jax.ShapeDtypeStruct((b, rows, HEAD_DIM), BF16),
        grid_spec=grid_spec,
        compiler_params=_cparams("parallel", "arbitrary"),
        name="sb_sample",
    )(page_table, q_rows, bias_rows, k_new, v_new, *([cache_k] * group), *([cache_v] * group), u_mat, norm_rows)


def _gdn_kernel(xq_ref, xk_ref, xv_ref, z_ref, ab_ref, c0q_ref, c0k_ref, c0v_ref, cwq_ref, cwk_ref, cwv_ref,
                alog_ref, dtb_ref, nw_ref, s0_ref, ltri_ref, lvl_ref, ob_ref, sfin_ref,
                s_scr, bq_scr, bk_scr, bv_scr, *, t_valid, n_chunks):
    c = pl.program_id(1)
    ch = GDN_CHUNK

    @pl.when(c == 0)
    def _():
        s_scr[...] = s0_ref[...]
        bq_scr[0:HALO, :] = c0q_ref[...]
        bk_scr[0:HALO, :] = c0k_ref[...]
        bv_scr[0:HALO, :] = c0v_ref[...]

    def conv_silu(x_ref, buf, cw_ref):
        buf[HALO:HALO + ch, :] = x_ref[...]
        first = HALO - (CONV_TAPS - 1)
        y = buf[first:first + ch, :] * cw_ref[0:1, :]
        for tap in range(1, CONV_TAPS):
            y = y + buf[first + tap:first + tap + ch, :] * cw_ref[tap:tap + 1, :]
        buf[0:HALO, :] = buf[ch:ch + HALO, :]
        return y * _sigmoid(y)

    yq = conv_silu(xq_ref, bq_scr, cwq_ref)
    yk = conv_silu(xk_ref, bk_scr, cwk_ref)
    yv = conv_silu(xv_ref, bv_scr, cwv_ref)

    ab = ab_ref[...]
    tok = c * ch + lax.broadcasted_iota(jnp.int32, ab.shape, 0)
    valid = tok < t_valid
    log_g = jnp.where(valid, -jnp.exp(alog_ref[...]) * _softplus(ab + dtb_ref[...]), 0.0)
    beta = jnp.where(valid, _sigmoid(ab), 0.0)
    ltri = ltri_ref[...]
    g_hi, g_mid, g_lo = _split3(log_g)
    gam_c = _dot(ltri, g_hi) + (_dot(ltri, g_mid) + _dot(ltri, g_lo))
    gam_last = gam_c[ch - 1:ch, :]
    e_gam = jnp.exp(gam_c)
    e_rest = jnp.exp(gam_last - gam_c)
    g_tot = jnp.exp(gam_last)

    heads = range(N_HEADS)
    quads = range(N_HEADS // QUAD)
    hsl = [slice(h * HEAD_DIM, (h + 1) * HEAD_DIM) for h in heads]
    qsl = [slice(g * QUAD * HEAD_DIM, (g + 1) * QUAD * HEAD_DIM) for g in quads]
    lsl = [slice((h % QUAD) * HEAD_DIM, (h % QUAD + 1) * HEAD_DIM) for h in heads]

    def per_head(mat, first_col):
        return jnp.concatenate([jnp.broadcast_to(mat[:, first_col + h:first_col + h + 1], (ch, HEAD_DIM))
                                for h in heads], axis=1)

    def l2_normalize(y):
        return jnp.concatenate([y[:, sl] * lax.rsqrt(jnp.sum(y[:, sl] * y[:, sl], axis=-1, keepdims=True) + NORM_EPS)
                                for sl in hsl], axis=1)

    beta_x = per_head(beta, N_HEADS)
    e_gam_x = per_head(e_gam, 0)
    q = l2_normalize(yq) * HEAD_DIM ** -0.5
    k = l2_normalize(yk)
    k_beta = k * beta_x
    v_beta = yv * beta_x
    k_beta_gam = k_beta * e_gam_x
    q_gam = (q * e_gam_x).astype(BF16)
    k_rest = (k * per_head(e_rest, 0)).astype(BF16)

    qw = QUAD * ch
    ri = lax.broadcasted_iota(jnp.int32, (ch, qw), 0)
    li = lax.broadcasted_iota(jnp.int32, (ch, qw), 1)
    ci = li & (ch - 1)
    incl = ri >= ci
    strict = ri > ci
    eye = jnp.where(ri == ci, 1.0, 0.0).astype(F32)

    a_mat, attn, inv = [], [], []
    for g in quads:
        gam_q = jnp.broadcast_to(gam_c[:, QUAD * g + QUAD - 1:QUAD * g + QUAD], (ch, qw))
        for j in range(QUAD - 2, -1, -1):
            gam_q = jnp.where(li < (j + 1) * ch, jnp.broadcast_to(gam_c[:, QUAD * g + j:QUAD * g + j + 1], (ch, qw)),
                              gam_q)
        diff = gam_q - jnp.sum(gam_q * eye, axis=0, keepdims=True)
        decay = jnp.where(incl, jnp.exp(jnp.where(incl, diff, 0.0)), 0.0)
        k_diag = _lane_block_diag(k[:, qsl[g]], QUAD).astype(BF16)
        a_mat.append(jnp.where(strict, _dot_nt(k_beta[:, qsl[g]].astype(BF16), k_diag) * decay, 0.0))
        attn.append((_dot_nt(q[:, qsl[g]].astype(BF16), k_diag) * decay).astype(BF16))
        inv.append(eye - a_mat[g] * lvl_ref[0])
    for lvl in range(1, INV_LEVELS):
        inner = [_mm3_block_diag(a_mat[g] * lvl_ref[lvl], inv[g], QUAD) for g in quads]
        inv = [inv[g] - _mm3_block_diag(inv[g], inner[g], QUAD) for g in quads]
    u = [_mm3_block_diag(inv[g], v_beta[:, qsl[g]], QUAD) for g in quads]
    w = [_mm3_block_diag(inv[g], k_beta_gam[:, qsl[g]], QUAD) for g in quads]
    s_old = [s_scr[h] for h in heads]
    s_bf = [s.astype(BF16) for s in s_old]
    v_new = [u[h // QUAD][:, lsl[h]] - _dot(w[h // QUAD][:, lsl[h]].astype(BF16), s_bf[h]) for h in heads]
    o_state = [_dot(q_gam[:, hsl[h]], s_bf[h]) for h in heads]
    for h in heads:
        s_scr[h] = s_old[h] * g_tot[:, h:h + 1] + _dot_tn(k_rest[:, hsl[h]], v_new[h].astype(BF16))
    o_chunk = [_dot(attn[g], _lane_block_diag(jnp.concatenate(v_new[QUAD * g:QUAD * (g + 1)], axis=1),
                                              QUAD).astype(BF16)) for g in quads]
    for h in heads:
        o = o_state[h] + o_chunk[h // QUAD][:, lsl[h]]
        z = z_ref[:, hsl[h]]
        ob_ref[:, hsl[h]] = (_rms_scale(o) * nw_ref[...] * (z * _sigmoid(z))).astype(ob_ref.dtype)

    @pl.when(c == n_chunks - 1)
    def _():
        sfin_ref[...] = s_scr[...]


def _gdn(proj3, ab3, conv0, conv_w, alog_row, dtb_row, norm_w, s0, ltri, lvl_masks, t_valid):
    b, t, _ = proj3.shape
    ch = GDN_CHUNK
    n_chunks = t // ch
    gw = GROUP_WIDTH

    def col_spec(rows, blk_col):
        return pl.BlockSpec((None, rows, gw), lambda bi, c: (bi, c if rows == ch else 0, blk_col))

    def cw_spec(blk_col):
        return pl.BlockSpec((HALO, gw), lambda bi, c: (0, blk_col))

    row_spec = pl.BlockSpec((1, HEAD_DIM), lambda bi, c: (0, 0))
    state_spec = pl.BlockSpec((None, N_HEADS, HEAD_DIM, HEAD_DIM), lambda bi, c: (bi, 0, 0, 0))
    return pl.pallas_call(
        functools.partial(_gdn_kernel, t_valid=t_valid, n_chunks=n_chunks),
        out_shape=(jax.ShapeDtypeStruct((b, t, gw), BF16),
                   jax.ShapeDtypeStruct((b, N_HEADS, HEAD_DIM, HEAD_DIM), F32)),
        grid=(b, n_chunks),
        in_specs=[col_spec(ch, 3), col_spec(ch, 4), col_spec(ch, 5), col_spec(ch, 6),
                  pl.BlockSpec((None, ch, HEAD_DIM), lambda bi, c: (bi, c, 0)),
                  col_spec(HALO, 0), col_spec(HALO, 1), col_spec(HALO, 2),
                  cw_spec(0), cw_spec(1), cw_spec(2),
                  row_spec, row_spec, row_spec, state_spec,
                  pl.BlockSpec((ch, ch), lambda bi, c: (0, 0)),
                  pl.BlockSpec((INV_LEVELS, ch, QUAD * ch), lambda bi, c: (0, 0, 0))],
        out_specs=(pl.BlockSpec((None, ch, gw), lambda bi, c: (bi, c, 0)), state_spec),
        scratch_shapes=[pltpu.VMEM((N_HEADS, HEAD_DIM, HEAD_DIM), F32),
                        pltpu.VMEM((HALO + ch, gw), F32),
                        pltpu.VMEM((HALO + ch, gw), F32),
                        pltpu.VMEM((HALO + ch, gw), F32)],
        compiler_params=_cparams("parallel", "arbitrary"),
        name="gdn",
    )(proj3, proj3, proj3, proj3, ab3, conv0, conv0, conv0, conv_w, conv_w, conv_w,
      alog_row, dtb_row, norm_w, s0, ltri, lvl_masks)


def _outproj_kernel(h_ref, oa_ref, ob_ref, wa_ref, wb_ref, o_ref):
    o_ref[...] = h_ref[...] + (_dot(oa_ref[...], wa_ref[...]) + _dot(ob_ref[...], wb_ref[...]))


def _outproj(h, oa, ob, w_out, bm):
    m, d = h.shape
    gw = GROUP_WIDTH
    return pl.pallas_call(
        _outproj_kernel,
        out_shape=jax.ShapeDtypeStruct((m, d), F32),
        grid=(m // bm,),
        in_specs=[pl.BlockSpec((bm, d), lambda i: (i, 0)),
                  pl.BlockSpec((bm, gw), lambda i: (i, 0)),
                  pl.BlockSpec((bm, gw), lambda i: (i, 0)),
                  pl.BlockSpec((gw, d), lambda i: (0, 0)),
                  pl.BlockSpec((gw, d), lambda i: (1, 0))],
        out_specs=pl.BlockSpec((bm, d), lambda i: (i, 0)),
        compiler_params=_cparams("parallel"),
        name="outproj",
    )(h, oa, ob, w_out, w_out)


def _ffn_kernel(h_ref, nw_ref, wg_ref, wu_ref, wd_ref, fnw_ref, o_ref, xn_ref, acc_ref, *, n_f, final_norm):
    f = pl.program_id(1)

    @pl.when(f == 0)
    def _():
        xn_ref[...] = (_rms_scale(h_ref[...]) * nw_ref[...]).astype(BF16)
        acc_ref[...] = jnp.zeros_like(acc_ref)

    xn = xn_ref[...]
    g = _dot(xn, wg_ref[...])
    act = (g * _sigmoid(g)) * _dot(xn, wu_ref[...])
    acc_ref[...] += _dot(act.astype(BF16), wd_ref[...])

    @pl.when(f == n_f - 1)
    def _():
        out = h_ref[...] + acc_ref[...]
        if final_norm:
            out = _rms_scale(out) * fnw_ref[...]
        o_ref[...] = out


def _ffn(h, norm_w, w_gate, w_up, w_down, final_w, bm, bf, final_norm):
    m, d = h.shape
    f_dim = w_gate.shape[1]
    n_f = f_dim // bf
    return pl.pallas_call(
        functools.partial(_ffn_kernel, n_f=n_f, final_norm=final_norm),
        out_shape=jax.ShapeDtypeStruct((m, d), F32),
        grid=(m // bm, n_f),
        in_specs=[pl.BlockSpec((bm, d), lambda i, f: (i, 0)),
                  pl.BlockSpec((1, d), lambda i, f: (0, 0)),
                  pl.BlockSpec((d, bf), lambda i, f: (0, f)),
                  pl.BlockSpec((d, bf), lambda i, f: (0, f)),
                  pl.BlockSpec((bf, d), lambda i, f: (f, 0)),
                  pl.BlockSpec((1, d), lambda i, f: (0, 0))],
        out_specs=pl.BlockSpec((bm, d), lambda i, f: (i, 0)),
        scratch_shapes=[pltpu.VMEM((bm, d), BF16), pltpu.VMEM((bm, d), F32)],
        compiler_params=_cparams("parallel", "arbitrary"),
        name="ffn",
    )(h, norm_w, w_gate, w_up, w_down, final_w)


def _row_block(m, target):
    return target if m % target == 0 else m


def _excl_upper(n):
    j = lax.broadcasted_iota(jnp.int32, (n, n), 0)
    s = lax.broadcasted_iota(jnp.int32, (n, n), 1)
    return (j > s).astype(BF16)


def _pad_lanes(row, width):
    return jnp.pad(row, (0, width - row.shape[0])).reshape(1, width)


def kernel(x_prompt, x_sample, cache_k, cache_v, state_S, state_conv, page_table, norm_mix_w, w_in, conv_w, a_log, dt_bias, sb_bias, sb_norm_w, gdn_norm_w, w_out, norm_ffn_w, w_gate, w_up, w_down, final_norm_w):
    bp, tp, d = x_prompt.shape
    bs, ts, _ = x_sample.shape
    depth = w_in.shape[0]
    gw = GROUP_WIDTH
    main_cols = N_MAIN_BLOCKS * gw
    qkv_cols = 3 * gw
    ch = GDN_CHUNK
    assert w_in.shape[2] == main_cols + 2 * N_HEADS and tp % (2 * SB_BLOCK) == 0 and ts <= ch
    assert cache_k.shape[2:] == (PAGE_SIZE, N_HEADS, HEAD_DIM)

    w_in_bf = w_in.astype(BF16)
    w_ab_bf = jnp.pad(w_in_bf[:, :, main_cols:], ((0, 0), (0, 0), (0, HEAD_DIM - 2 * N_HEADS)))
    w_out_bf = w_out.astype(BF16)
    w_gate_bf = w_gate.astype(BF16)
    w_up_bf = w_up.astype(BF16)
    w_down_bf = w_down.astype(BF16)
    conv_w8 = jnp.pad(conv_w, ((0, 0), (0, HALO - CONV_TAPS), (0, 0)))
    cache_k = cache_k.reshape(cache_k.shape[:2] + (PAGE_SIZE * N_HEADS, HEAD_DIM))
    cache_v = cache_v.reshape(cache_v.shape[:2] + (PAGE_SIZE * N_HEADS, HEAD_DIM))

    u_prompt = _excl_upper(SB_BLOCK)
    u_page = _excl_upper(PAGE_SIZE)
    ri = lax.broadcasted_iota(jnp.int32, (ch, ch), 0)
    ci = lax.broadcasted_iota(jnp.int32, (ch, ch), 1)
    ltri = (ri >= ci).astype(BF16)
    lvl_masks = jnp.stack([((ri // (2 * b) == ci // (2 * b)) & ((ri & b) != 0) & ((ci & b) == 0)).astype(F32)
                           for b in (1 << e for e in range(INV_LEVELS))])
    lvl_masks = jnp.tile(lvl_masks, (1, 1, QUAD))
    head_eye = jnp.eye(N_HEADS, dtype=F32)
    final_w = final_norm_w.reshape(1, d)

    hp = x_prompt.reshape(bp * tp, d)
    hs = x_sample.reshape(bs * ts, d)
    bm_p = _row_block(bp * tp, 512)
    bm_s = bs * ts
    conv0_p = jnp.zeros((bp, HALO, qkv_cols), F32)
    s0_p = jnp.zeros((bp, N_HEADS, HEAD_DIM, HEAD_DIM), F32)

    outs = [[] for _ in range(8)]
    for l in range(depth):
        nmw = norm_mix_w[l].reshape(1, d)
        nfw = norm_ffn_w[l].reshape(1, d)
        alog_row = _pad_lanes(a_log[l], HEAD_DIM)
        dtb_row = _pad_lanes(dt_bias[l], HEAD_DIM)
        gnw = gdn_norm_w[l].reshape(1, HEAD_DIM)
        last = l == depth - 1

        proj, ab = _inproj(hp, nmw, w_in_bf[l], w_ab_bf[l], bm_p)
        proj3 = proj.reshape(bp, tp, main_cols)
        oa = _sb_prompt(proj3, sb_bias[l], u_prompt, sb_norm_w[l])
        ob, s_new = _gdn(proj3, ab.reshape(bp, tp, HEAD_DIM), conv0_p, conv_w8[l], alog_row, dtb_row, gnw,
                         s0_p, ltri, lvl_masks, tp)
        hp = _outproj(hp, oa.reshape(bp * tp, gw), ob.reshape(bp * tp, gw), w_out_bf[l], bm_p)
        hp = _ffn(hp, nfw, w_gate_bf[l], w_up_bf[l], w_down_bf[l], final_w, bm_p, 512, last)
        outs[0].append(proj3[:, :, gw:2 * gw].reshape(bp, tp, N_HEADS, HEAD_DIM))
        outs[1].append(proj3[:, :, 2 * gw:3 * gw].reshape(bp, tp, N_HEADS, HEAD_DIM))
        outs[4].append(s_new)
        outs[6].append(proj3[:, tp - (CONV_TAPS - 1):, 3 * gw:6 * gw])

        proj, ab = _inproj(hs, nmw, w_in_bf[l], w_ab_bf[l], bm_s)
        proj3 = proj.reshape(bs, ts, main_cols)
        k_new = proj3[:, :, gw:2 * gw]
        v_new = proj3[:, :, 2 * gw:3 * gw]
        q4 = proj3[:, :, :gw].reshape(bs, ts, N_HEADS, HEAD_DIM) * HEAD_DIM ** -0.5
        q_rows = jnp.einsum('bthd,hg->bhtgd', q4, head_eye).reshape(bs, N_HEADS * ts, gw).astype(BF16)
        bias_rows = jnp.broadcast_to(jnp.repeat(sb_bias[l], ts)[:, None], (N_HEADS * ts, PAGE_SIZE))
        norm_rows = jnp.repeat(sb_norm_w[l], ts, axis=0)
        pad_keys = ((0, 0), (0, PAGE_SIZE - ts), (0, 0))
        oa = _sb_sample(page_table, q_rows, bias_rows, jnp.pad(k_new, pad_keys), jnp.pad(v_new, pad_keys),
                        cache_k, cache_v, l, u_page, norm_rows)
        oa = oa.reshape(bs, N_HEADS, ts, HEAD_DIM).transpose(0, 2, 1, 3).reshape(bs * ts, gw)
        pad_chunk = ((0, 0), (0, ch - ts), (0, 0))
        conv0_s = jnp.pad(state_conv[l], ((0, 0), (HALO - (CONV_TAPS - 1), 0), (0, 0)))
        ob, s_new = _gdn(jnp.pad(proj3, pad_chunk), jnp.pad(ab.reshape(bs, ts, HEAD_DIM), pad_chunk), conv0_s,
                         conv_w8[l], alog_row, dtb_row, gnw, state_S[l], ltri, lvl_masks, ts)
        hs = _outproj(hs, oa, ob[:, :ts].reshape(bs * ts, gw), w_out_bf[l], bm_s)
        hs = _ffn(hs, nfw, w_gate_bf[l], w_up_bf[l], w_down_bf[l], final_w, bm_s, 512, last)
        outs[2].append(k_new.reshape(bs, ts, N_HEADS, HEAD_DIM))
        outs[3].append(v_new.reshape(bs, ts, N_HEADS, HEAD_DIM))
        outs[5].append(s_new)
        xc_tail = jnp.concatenate([state_conv[l], proj3[:, :, 3 * gw:6 * gw]], axis=1)
        outs[7].append(xc_tail[:, xc_tail.shape[1] - (CONV_TAPS - 1):])

    return (hp.reshape(bp, tp, d), hs.reshape(bs, ts, d)) + tuple(jnp.stack(o) for o in outs)
```

```python
import functools

import jax
import jax.numpy as jnp
from jax import lax
from jax.experimental import pallas as pl
from jax.experimental.pallas import tpu as pltpu

F32 = jnp.float32
BF16 = jnp.bfloat16

NORM_EPS = 1e-6
HEAD_DIM = 128
LOG2_E = 1.4426950408889634
SB_QUERY_SCALE = -LOG2_E * HEAD_DIM ** -0.5
N_HEADS = 8
GROUP_WIDTH = N_HEADS * HEAD_DIM
CONV_TAPS = 4
GDN_CHUNK = 64
INV_LEVELS = 6
PAGE_SIZE = 128
SB_BLOCK = 256
SAMPLE_PAGES_PER_STEP = 8
QUAD = 4
GDN_SEQS_PER_STEP = 4
HALO = 8
N_MAIN_BLOCKS = 7
VMEM_LIMIT_BYTES = 48 * 1024 * 1024


def _cparams(*semantics):
    return pltpu.CompilerParams(dimension_semantics=semantics, vmem_limit_bytes=VMEM_LIMIT_BYTES)


def _dot(a, b):
    return jnp.dot(a, b, preferred_element_type=F32)


def _dot_nt(a, b):
    return lax.dot_general(a, b, (((1,), (1,)), ((), ())), preferred_element_type=F32)


def _dot_tn(a, b):
    return lax.dot_general(a, b, (((0,), (0,)), ((), ())), preferred_element_type=F32)


def _split2(x):
    hi = x.astype(BF16)
    lo = (x - hi.astype(F32)).astype(BF16)
    return hi, lo


def _split3(x):
    hi = x.astype(BF16)
    r = x - hi.astype(F32)
    mid = r.astype(BF16)
    lo = (r - mid.astype(F32)).astype(BF16)
    return hi, mid, lo


def _lane_block_masks(rows, n, w):
    lane = lax.broadcasted_iota(jnp.int32, (rows, n * w), 1)
    return [jnp.where((lane >= j * w) & (lane < (j + 1) * w), 1.0, 0.0).astype(BF16) for j in range(n)]


def _lane_block_diag(x, masks):
    return jnp.concatenate([x * m for m in masks], axis=0)


def _mm3_block_diag(x, y, masks):
    yh, yl = _split2(y)
    xh, xl = _split2(x)
    m = x.shape[0]
    both = _dot(jnp.concatenate([xh, xl], axis=0), _lane_block_diag(yh, masks))
    return both[:m] + (_dot(xh, _lane_block_diag(yl, masks)) + both[m:])


def _sigmoid(x):
    return 1.0 / (1.0 + jnp.exp(-x))


def _softplus(x):
    return jnp.maximum(x, 0.0) + jnp.log1p(jnp.exp(-jnp.abs(x)))


def _rms_scale(x):
    return x * lax.rsqrt(jnp.mean(x * x, axis=-1, keepdims=True) + NORM_EPS)


def _inproj_kernel(x_ref, nw_ref, w_ref, wab_ref, proj_ref, ab_ref, xn_ref):
    @pl.when(pl.program_id(1) == 0)
    def _():
        xn = (_rms_scale(x_ref[...]) * nw_ref[...]).astype(BF16)
        xn_ref[...] = xn
        ab_ref[...] = _dot(xn, wab_ref[...])

    proj_ref[...] = _dot(xn_ref[...], w_ref[...])


def _inproj(x, norm_w, w_main, w_ab, bm):
    m, d = x.shape
    return pl.pallas_call(
        _inproj_kernel,
        out_shape=(jax.ShapeDtypeStruct((m, N_MAIN_BLOCKS * GROUP_WIDTH), F32),
                   jax.ShapeDtypeStruct((m, HEAD_DIM), F32)),
        grid=(m // bm, N_MAIN_BLOCKS),
        in_specs=[pl.BlockSpec((bm, d), lambda i, j: (i, 0)),
                  pl.BlockSpec((1, d), lambda i, j: (0, 0)),
                  pl.BlockSpec((d, GROUP_WIDTH), lambda i, j: (0, j)),
                  pl.BlockSpec((d, HEAD_DIM), lambda i, j: (0, 0))],
        out_specs=(pl.BlockSpec((bm, GROUP_WIDTH), lambda i, j: (i, j)),
                   pl.BlockSpec((bm, HEAD_DIM), lambda i, j: (i, 0))),
        scratch_shapes=[pltpu.VMEM((bm, d), BF16)],
        compiler_params=_cparams("parallel", "arbitrary"),
        name="inproj",
    )(x, norm_w, w_main, w_ab)


def _sb_block(q_neg, kb, vb, u_mat, bias_neg, carry, mask):
    n = _dot_nt(q_neg, kb) + bias_neg
    log_not = jnp.minimum(n, 0.0) - jnp.log2(1.0 + jnp.exp2(-jnp.abs(n)))
    log_beta = log_not - n
    if mask is not None:
        log_not = jnp.where(mask, log_not, 0.0)
    after = _dot(log_not.astype(BF16), u_mat) + carry
    w = jnp.exp2(log_beta + after)
    if mask is not None:
        w = jnp.where(mask, w, 0.0)
    pv = _dot(w.astype(BF16), vb)
    return pv, carry + jnp.sum(log_not, axis=-1, keepdims=True)


def _sb_prompt_kernel(bias_ref, q_ref, k_ref, v_ref, u_ref, nw_ref, o_ref, *, blk):
    h = pl.program_id(1)
    i = pl.program_id(2)
    bias = bias_ref[h] * -LOG2_E
    q0 = (q_ref[0:blk, :] * SB_QUERY_SCALE).astype(BF16)
    q1 = (q_ref[blk:2 * blk, :] * SB_QUERY_SCALE).astype(BF16)
    u_mat = u_ref[...]
    row = lax.broadcasted_iota(jnp.int32, (blk, blk), 0)
    col = lax.broadcasted_iota(jnp.int32, (blk, blk), 1)

    def kv_block(j):
        start = pl.multiple_of(j * blk, blk)
        return k_ref[pl.ds(start, blk), :].astype(BF16), v_ref[pl.ds(start, blk), :].astype(BF16)

    mask = col < row
    zero = jnp.zeros((blk, 1), F32)
    kb, vb = kv_block(2 * i + 1)
    acc1, carry1 = _sb_block(q1, kb, vb, u_mat, bias, zero, mask)
    kb, vb = kv_block(2 * i)
    acc0, carry0 = _sb_block(q0, kb, vb, u_mat, bias, zero, mask)
    pv, carry1 = _sb_block(q1, kb, vb, u_mat, bias, carry1, None)
    acc1 = acc1 + pv

    def body(n, state):
        acc0, carry0, acc1, carry1 = state
        kb, vb = kv_block(2 * i - 1 - n)
        pv0, carry0 = _sb_block(q0, kb, vb, u_mat, bias, carry0, None)
        pv1, carry1 = _sb_block(q1, kb, vb, u_mat, bias, carry1, None)
        return acc0 + pv0, carry0, acc1 + pv1, carry1

    acc0, _, acc1, _ = lax.fori_loop(0, 2 * i, body, (acc0, carry0, acc1, carry1))
    norm_w = nw_ref[pl.ds(h, 1), :]
    o_ref[0:blk, :] = (_rms_scale(acc0) * norm_w).astype(o_ref.dtype)
    o_ref[blk:2 * blk, :] = (_rms_scale(acc1) * norm_w).astype(o_ref.dtype)


def _sb_prompt(proj3, sb_bias, u_mat, norm_w):
    b, t, _ = proj3.shape
    blk = SB_BLOCK
    return pl.pallas_call(
        functools.partial(_sb_prompt_kernel, blk=blk),
        out_shape=jax.ShapeDtypeStruct((b, t, GROUP_WIDTH), BF16),
        grid=(b, N_HEADS, t // (2 * blk)),
        in_specs=[pl.BlockSpec(memory_space=pltpu.SMEM),
                  pl.BlockSpec((None, 2 * blk, HEAD_DIM), lambda bi, h, i: (bi, i, h)),
                  pl.BlockSpec((None, t, HEAD_DIM), lambda bi, h, i: (bi, 0, N_HEADS + h)),
                  pl.BlockSpec((None, t, HEAD_DIM), lambda bi, h, i: (bi, 0, 2 * N_HEADS + h)),
                  pl.BlockSpec((blk, blk), lambda bi, h, i: (0, 0)),
                  pl.BlockSpec((N_HEADS, HEAD_DIM), lambda bi, h, i: (0, 0))],
        out_specs=pl.BlockSpec((None, 2 * blk, HEAD_DIM), lambda bi, h, i: (bi, i, h)),
        compiler_params=_cparams("parallel", "parallel", "arbitrary"),
        name="sb_prompt",
    )(sb_bias, proj3, proj3, proj3, u_mat, norm_w)


def _page_rows(page_ref):
    return jnp.concatenate([page_ref[pl.ds(h, PAGE_SIZE, stride=N_HEADS), :] for h in range(N_HEADS)],
                           axis=1).astype(BF16)


def _sb_sample_kernel(pt_ref, q_ref, bias_ref, knew_ref, vnew_ref, *rest, n_steps, t_new, group):
    del pt_ref
    k_refs, v_refs = rest[:group], rest[group:2 * group]
    u_ref, nw_ref, o_ref, acc_ref, carry_ref = rest[2 * group:]
    p = pl.program_id(1)
    rows = N_HEADS * t_new
    q = q_ref[...]
    bias = bias_ref[...]
    u_mat = u_ref[...]
    row = lax.broadcasted_iota(jnp.int32, (rows, PAGE_SIZE), 0)
    col = lax.broadcasted_iota(jnp.int32, (rows, PAGE_SIZE), 1)

    @pl.when(p == 0)
    def _():
        pv, carry = _sb_block(q, knew_ref[...].astype(BF16), vnew_ref[...].astype(BF16), u_mat, bias,
                              jnp.zeros((rows, 1), F32), col < lax.rem(row, t_new))
        acc_ref[...] = pv
        carry_ref[...] = jnp.broadcast_to(carry, carry_ref.shape)

    @pl.when(p > 0)
    def _():
        carry = carry_ref[:, 0:1]
        acc = acc_ref[...]
        for g in range(group):
            pv, carry = _sb_block(q, _page_rows(k_refs[g]), _page_rows(v_refs[g]), u_mat, bias, carry, None)
            acc = acc + pv
        acc_ref[...] = acc
        carry_ref[...] = jnp.broadcast_to(carry, carry_ref.shape)

    @pl.when(p == n_steps)
    def _():
        o = jnp.zeros((rows, HEAD_DIM), F32)
        for h in range(N_HEADS):
            in_head = (row >= h * t_new) & (row < (h + 1) * t_new)
            o = o + jnp.where(in_head, acc_ref[:, h * HEAD_DIM:(h + 1) * HEAD_DIM], 0.0)
        o_ref[...] = (_rms_scale(o) * nw_ref[...]).astype(o_ref.dtype)


def _sb_sample(page_table, q_rows, bias_rows, k_new, v_new, cache_k, cache_v, layer, u_mat, norm_rows):
    b, rows, _ = q_rows.shape
    n_pages = page_table.shape[1]
    t_new = rows // N_HEADS
    group = max(g for g in range(1, SAMPLE_PAGES_PER_STEP + 1) if n_pages % g == 0)
    n_steps = n_pages // group

    def page_spec(g):
        def page_map(bi, p, pt):
            return (layer, pt[bi, n_pages - 1 - (jnp.maximum(p - 1, 0) * group + g)], 0, 0)
        return pl.BlockSpec((None, None, PAGE_SIZE * N_HEADS, HEAD_DIM), page_map)

    page_specs = [page_spec(g) for g in range(group)]
    grid_spec = pltpu.PrefetchScalarGridSpec(
        num_scalar_prefetch=1,
        grid=(b, n_steps + 1),
        in_specs=[pl.BlockSpec((None, rows, GROUP_WIDTH), lambda bi, p, pt: (bi, 0, 0)),
                  pl.BlockSpec((rows, PAGE_SIZE), lambda bi, p, pt: (0, 0)),
                  pl.BlockSpec((None, PAGE_SIZE, GROUP_WIDTH), lambda bi, p, pt: (bi, 0, 0)),
                  pl.BlockSpec((None, PAGE_SIZE, GROUP_WIDTH), lambda bi, p, pt: (bi, 0, 0))]
                 + page_specs + page_specs
                 + [pl.BlockSpec((PAGE_SIZE, PAGE_SIZE), lambda bi, p, pt: (0, 0)),
                    pl.BlockSpec((rows, HEAD_DIM), lambda bi, p, pt: (0, 0))],
        out_specs=pl.BlockSpec((None, rows, HEAD_DIM), lambda bi, p, pt: (bi, 0, 0)),
        scratch_shapes=[pltpu.VMEM((rows, GROUP_WIDTH), F32), pltpu.VMEM((rows, PAGE_SIZE), F32)],
    )
    return pl.pallas_call(
        functools.partial(_sb_sample_kernel, n_steps=n_steps, t_new=t_new, group=group),
        out_shape=jax.ShapeDtypeStruct((b, rows, HEAD_DIM), BF16),
        grid_spec=grid_spec,
        compiler_params=_cparams("parallel", "arbitrary"),
        name="sb_sample",
    )(page_table, q_rows, bias_rows, k_new, v_new, *([cache_k] * group), *([cache_v] * group), u_mat, norm_rows)


def _gdn_kernel(xq_ref, xk_ref, xv_ref, z_ref, ab_ref, c0q_ref, c0k_ref, c0v_ref, cwq_ref, cwk_ref, cwv_ref,
                alog_ref, dtb_ref, nw_ref, s0_ref, ltri_ref, lvl_ref, ob_ref, sfin_ref,
                s_scr, bq_scr, bk_scr, bv_scr, *, t_valid, n_chunks, n_seq):
    c = pl.program_id(1)
    ch = GDN_CHUNK
    seqs = range(n_seq)
    heads = range(n_seq * N_HEADS)
    quads = range(n_seq * N_HEADS // QUAD)

    @pl.when(c == 0)
    def _():
        for n in seqs:
            for h in range(N_HEADS):
                s_scr[n * N_HEADS + h] = s0_ref[n, h]
            bq_scr[n, 0:HALO, :] = c0q_ref[n]
            bk_scr[n, 0:HALO, :] = c0k_ref[n]
            bv_scr[n, 0:HALO, :] = c0v_ref[n]

    def conv_silu(x_ref, buf, cw_ref, n):
        buf[n, HALO:HALO + ch, :] = x_ref[n]
        x = buf[n]
        y = pltpu.roll(x, CONV_TAPS - 1, axis=0)[HALO:, :] * cw_ref[0:1, :]
        for tap in range(1, CONV_TAPS - 1):
            y = y + pltpu.roll(x, CONV_TAPS - 1 - tap, axis=0)[HALO:, :] * cw_ref[tap:tap + 1, :]
        y = y + x[HALO:, :] * cw_ref[CONV_TAPS - 1:CONV_TAPS, :]
        buf[n, 0:HALO, :] = x[ch:ch + HALO, :]
        return y * _sigmoid(y)

    yq = jnp.concatenate([conv_silu(xq_ref, bq_scr, cwq_ref, n) for n in seqs], axis=1)
    yk = jnp.concatenate([conv_silu(xk_ref, bk_scr, cwk_ref, n) for n in seqs], axis=1)
    yv = jnp.concatenate([conv_silu(xv_ref, bv_scr, cwv_ref, n) for n in seqs], axis=1)

    ltri = ltri_ref[...]
    tok = c * ch + lax.broadcasted_iota(jnp.int32, (ch, HEAD_DIM), 0)
    valid = tok < t_valid
    beta, gam_c, e_gam, e_rest, g_tot = [], [], [], [], []
    for n in seqs:
        ab = ab_ref[n]
        log_g = jnp.where(valid, -jnp.exp(alog_ref[...]) * _softplus(ab + dtb_ref[...]), 0.0)
        beta.append(jnp.where(valid, _sigmoid(ab), 0.0))
        g_hi, g_mid, g_lo = _split3(log_g)
        gam = _dot(ltri, g_hi) + (_dot(ltri, g_mid) + _dot(ltri, g_lo))
        gam_last = gam[ch - 1:ch, :]
        gam_c.append(gam)
        e_gam.append(jnp.exp(gam))
        e_rest.append(jnp.exp(gam_last - gam))
        g_tot.append(jnp.exp(gam_last))

    hsl = [slice(h * HEAD_DIM, (h + 1) * HEAD_DIM) for h in heads]
    qsl = [slice(g * QUAD * HEAD_DIM, (g + 1) * QUAD * HEAD_DIM) for g in quads]
    lsl = [slice((h % QUAD) * HEAD_DIM, (h % QUAD + 1) * HEAD_DIM) for h in heads]

    def head_col(mats, h, first_col=0):
        col = first_col + h % N_HEADS
        return mats[h // N_HEADS][:, col:col + 1]

    def per_head(mats, first_col):
        return jnp.concatenate([jnp.broadcast_to(head_col(mats, h, first_col), (ch, HEAD_DIM)) for h in heads], axis=1)

    def l2_normalize(y):
        return jnp.concatenate([y[:, sl] * lax.rsqrt(jnp.sum(y[:, sl] * y[:, sl], axis=-1, keepdims=True) + NORM_EPS)
                                for sl in hsl], axis=1)

    beta_x = per_head(beta, N_HEADS)
    e_gam_x = per_head(e_gam, 0)
    q = l2_normalize(yq) * HEAD_DIM ** -0.5
    k = l2_normalize(yk)
    k_beta = k * beta_x
    v_beta = yv * beta_x
    k_beta_gam = k_beta * e_gam_x
    q_gam = (q * e_gam_x).astype(BF16)
    k_rest = (k * per_head(e_rest, 0)).astype(BF16)

    qw = QUAD * ch
    ri = lax.broadcasted_iota(jnp.int32, (ch, qw), 0)
    li = lax.broadcasted_iota(jnp.int32, (ch, qw), 1)
    ci = li & (ch - 1)
    incl = ri >= ci
    strict = ri > ci
    eye = jnp.where(ri == ci, 1.0, 0.0).astype(F32)
    masks_mat = _lane_block_masks(ch, QUAD, ch)
    masks_vec = _lane_block_masks(ch, QUAD, HEAD_DIM)

    a_mat, attn, inv = [], [], []
    for g in quads:
        gam_q = jnp.broadcast_to(head_col(gam_c, QUAD * g + QUAD - 1), (ch, qw))
        for j in range(QUAD - 2, -1, -1):
            gam_q = jnp.where(li < (j + 1) * ch, jnp.broadcast_to(head_col(gam_c, QUAD * g + j), (ch, qw)), gam_q)
        diff = gam_q - jnp.sum(gam_q * eye, axis=0, keepdims=True)
        decay = jnp.where(incl, jnp.exp(jnp.where(incl, diff, 0.0)), 0.0)
        k_diag = _lane_block_diag(k[:, qsl[g]].astype(BF16), masks_vec)
        a_mat.append(jnp.where(strict, _dot_nt(k_beta[:, qsl[g]].astype(BF16), k_diag) * decay, 0.0))
        attn.append((_dot_nt(q[:, qsl[g]].astype(BF16), k_diag) * decay).astype(BF16))
        inv.append(eye - a_mat[g] * lvl_ref[0])
    for lvl in range(1, INV_LEVELS):
        inner = [_mm3_block_diag(a_mat[g] * lvl_ref[lvl], inv[g], masks_mat) for g in quads]
        inv = [inv[g] - _mm3_block_diag(inv[g], inner[g], masks_mat) for g in quads]
    u = [_mm3_block_diag(inv[g], v_beta[:, qsl[g]], masks_vec) for g in quads]
    w = [_mm3_block_diag(inv[g], k_beta_gam[:, qsl[g]], masks_vec) for g in quads]
    s_old = [s_scr[h] for h in heads]
    s_bf = [s.astype(BF16) for s in s_old]
    v_new = [u[h // QUAD][:, lsl[h]] - _dot(w[h // QUAD][:, lsl[h]].astype(BF16), s_bf[h]) for h in heads]
    o_state = [_dot(q_gam[:, hsl[h]], s_bf[h]) for h in heads]
    v_new_bf = [v.astype(BF16) for v in v_new]
    for h in heads:
        s_scr[h] = s_old[h] * head_col(g_tot, h) + _dot_tn(k_rest[:, hsl[h]], v_new_bf[h])
    o_chunk = [_dot(attn[g], _lane_block_diag(jnp.concatenate(v_new_bf[QUAD * g:QUAD * (g + 1)], axis=1), masks_vec))
               for g in quads]
    for h in heads:
        o = o_state[h] + o_chunk[h // QUAD][:, lsl[h]]
        n, sl = h // N_HEADS, hsl[h % N_HEADS]
        z = z_ref[n, :, sl]
        ob_ref[n, :, sl] = (_rms_scale(o) * nw_ref[...] * (z * _sigmoid(z))).astype(ob_ref.dtype)

    @pl.when(c == n_chunks - 1)
    def _():
        for n in seqs:
            for h in range(N_HEADS):
                sfin_ref[n, h] = s_scr[n * N_HEADS + h]


def _gdn(proj3, ab3, conv0, conv_w, alog_row, dtb_row, norm_w, s0, ltri, lvl_masks, t_valid):
    b, t, _ = proj3.shape
    ch = GDN_CHUNK
    n_chunks = t // ch
    gw = GROUP_WIDTH
    n_seq = GDN_SEQS_PER_STEP if b % GDN_SEQS_PER_STEP == 0 else 1

    def col_spec(rows, blk_col):
        return pl.BlockSpec((n_seq, rows, gw), lambda bi, c: (bi, c if rows == ch else 0, blk_col))

    def cw_spec(blk_col):
        return pl.BlockSpec((HALO, gw), lambda bi, c: (0, blk_col))

    row_spec = pl.BlockSpec((1, HEAD_DIM), lambda bi, c: (0, 0))
    state_spec = pl.BlockSpec((n_seq, N_HEADS, HEAD_DIM, HEAD_DIM), lambda bi, c: (bi, 0, 0, 0))
    return pl.pallas_call(
        functools.partial(_gdn_kernel, t_valid=t_valid, n_chunks=n_chunks, n_seq=n_seq),
        out_shape=(jax.ShapeDtypeStruct((b, t, gw), BF16),
                   jax.ShapeDtypeStruct((b, N_HEADS, HEAD_DIM, HEAD_DIM), F32)),
        grid=(b // n_seq, n_chunks),
        in_specs=[col_spec(ch, 3), col_spec(ch, 4), col_spec(ch, 5), col_spec(ch, 6),
                  pl.BlockSpec((n_seq, ch, HEAD_DIM), lambda bi, c: (bi, c, 0)),
                  col_spec(HALO, 0), col_spec(HALO, 1), col_spec(HALO, 2),
                  cw_spec(0), cw_spec(1), cw_spec(2),
                  row_spec, row_spec, row_spec, state_spec,
                  pl.BlockSpec((ch, ch), lambda bi, c: (0, 0)),
                  pl.BlockSpec((INV_LEVELS, ch, QUAD * ch), lambda bi, c: (0, 0, 0))],
        out_specs=(pl.BlockSpec((n_seq, ch, gw), lambda bi, c: (bi, c, 0)), state_spec),
        scratch_shapes=[pltpu.VMEM((n_seq * N_HEADS, HEAD_DIM, HEAD_DIM), F32),
                        pltpu.VMEM((n_seq, HALO + ch, gw), F32),
                        pltpu.VMEM((n_seq, HALO + ch, gw), F32),
                        pltpu.VMEM((n_seq, HALO + ch, gw), F32)],
        compiler_params=_cparams("parallel", "arbitrary"),
        name="gdn",
    )(proj3, proj3, proj3, proj3, ab3, conv0, conv0, conv0, conv_w, conv_w, conv_w,
      alog_row, dtb_row, norm_w, s0, ltri, lvl_masks)


def _outproj_kernel(h_ref, oa_ref, ob_ref, wa_ref, wb_ref, o_ref):
    o_ref[...] = h_ref[...] + (_dot(oa_ref[...], wa_ref[...]) + _dot(ob_ref[...], wb_ref[...]))


def _outproj(h, oa, ob, w_out, bm):
    m, d = h.shape
    gw = GROUP_WIDTH
    return pl.pallas_call(
        _outproj_kernel,
        out_shape=jax.ShapeDtypeStruct((m, d), F32),
        grid=(m // bm,),
        in_specs=[pl.BlockSpec((bm, d), lambda i: (i, 0)),
                  pl.BlockSpec((bm, gw), lambda i: (i, 0)),
                  pl.BlockSpec((bm, gw), lambda i: (i, 0)),
                  pl.BlockSpec((gw, d), lambda i: (0, 0)),
                  pl.BlockSpec((gw, d), lambda i: (1, 0))],
        out_specs=pl.BlockSpec((bm, d), lambda i: (i, 0)),
        compiler_params=_cparams("parallel"),
        name="outproj",
    )(h, oa, ob, w_out, w_out)


def _ffn_kernel(h_ref, nw_ref, wg_ref, wu_ref, wd_ref, fnw_ref, o_ref, xn_ref, acc_ref, *, n_f, final_norm):
    f = pl.program_id(1)

    @pl.when(f == 0)
    def _():
        xn_ref[...] = (_rms_scale(h_ref[...]) * nw_ref[...]).astype(BF16)
        acc_ref[...] = jnp.zeros_like(acc_ref)

    xn = xn_ref[...]
    g = _dot(xn, wg_ref[...])
    act = (g * _sigmoid(g)) * _dot(xn, wu_ref[...])
    acc_ref[...] += _dot(act.astype(BF16), wd_ref[...])

    @pl.when(f == n_f - 1)
    def _():
        out = h_ref[...] + acc_ref[...]
        if final_norm:
            out = _rms_scale(out) * fnw_ref[...]
        o_ref[...] = out


def _ffn(h, norm_w, w_gate, w_up, w_down, final_w, bm, bf, final_norm):
    m, d = h.shape
    f_dim = w_gate.shape[1]
    n_f = f_dim // bf
    return pl.pallas_call(
        functools.partial(_ffn_kernel, n_f=n_f, final_norm=final_norm),
        out_shape=jax.ShapeDtypeStruct((m, d), F32),
        grid=(m // bm, n_f),
        in_specs=[pl.BlockSpec((bm, d), lambda i, f: (i, 0)),
                  pl.BlockSpec((1, d), lambda i, f: (0, 0)),
                  pl.BlockSpec((d, bf), lambda i, f: (0, f)),
                  pl.BlockSpec((d, bf), lambda i, f: (0, f)),
                  pl.BlockSpec((bf, d), lambda i, f: (f, 0)),
                  pl.BlockSpec((1, d), lambda i, f: (0, 0))],
        out_specs=pl.BlockSpec((bm, d), lambda i, f: (i, 0)),
        scratch_shapes=[pltpu.VMEM((bm, d), BF16), pltpu.VMEM((bm, d), F32)],
        compiler_params=_cparams("parallel", "arbitrary"),
        name="ffn",
    )(h, norm_w, w_gate, w_up, w_down, final_w)


def _row_block(m, target):
    return target if m % target == 0 else m


def _excl_upper(n):
    j = lax.broadcasted_iota(jnp.int32, (n, n), 0)
    s = lax.broadcasted_iota(jnp.int32, (n, n), 1)
    return (j > s).astype(BF16)


def _pad_lanes(row, width):
    return jnp.pad(row, (0, width - row.shape[0])).reshape(1, width)


def kernel(x_prompt, x_sample, cache_k, cache_v, state_S, state_conv, page_table, norm_mix_w, w_in, conv_w, a_log, dt_bias, sb_bias, sb_norm_w, gdn_norm_w, w_out, norm_ffn_w, w_gate, w_up, w_down, final_norm_w):
    bp, tp, d = x_prompt.shape
    bs, ts, _ = x_sample.shape
    depth = w_in.shape[0]
    gw = GROUP_WIDTH
    main_cols = N_MAIN_BLOCKS * gw
    qkv_cols = 3 * gw
    ch = GDN_CHUNK
    assert w_in.shape[2] == main_cols + 2 * N_HEADS and tp % (2 * SB_BLOCK) == 0 and ts <= ch
    assert cache_k.shape[2:] == (PAGE_SIZE, N_HEADS, HEAD_DIM)

    w_in_bf = w_in.astype(BF16)
    w_ab_bf = jnp.pad(w_in_bf[:, :, main_cols:], ((0, 0), (0, 0), (0, HEAD_DIM - 2 * N_HEADS)))
    w_out_bf = w_out.astype(BF16)
    w_gate_bf = w_gate.astype(BF16)
    w_up_bf = w_up.astype(BF16)
    w_down_bf = w_down.astype(BF16)
    conv_w8 = jnp.pad(conv_w, ((0, 0), (0, HALO - CONV_TAPS), (0, 0)))
    cache_k = cache_k.reshape(cache_k.shape[:2] + (PAGE_SIZE * N_HEADS, HEAD_DIM))
    cache_v = cache_v.reshape(cache_v.shape[:2] + (PAGE_SIZE * N_HEADS, HEAD_DIM))

    u_prompt = _excl_upper(SB_BLOCK)
    u_page = _excl_upper(PAGE_SIZE)
    ri = lax.broadcasted_iota(jnp.int32, (ch, ch), 0)
    ci = lax.broadcasted_iota(jnp.int32, (ch, ch), 1)
    ltri = (ri >= ci).astype(BF16)
    lvl_masks = jnp.stack([((ri // (2 * b) == ci // (2 * b)) & ((ri & b) != 0) & ((ci & b) == 0)).astype(F32)
                           for b in (1 << e for e in range(INV_LEVELS))])
    lvl_masks = jnp.tile(lvl_masks, (1, 1, QUAD))
    head_eye = jnp.eye(N_HEADS, dtype=F32)
    final_w = final_norm_w.reshape(1, d)

    hp = x_prompt.reshape(bp * tp, d)
    hs = x_sample.reshape(bs * ts, d)
    bm_p = _row_block(bp * tp, 512)
    bm_s = bs * ts
    conv0_p = jnp.zeros((bp, HALO, qkv_cols), F32)
    s0_p = jnp.zeros((bp, N_HEADS, HEAD_DIM, HEAD_DIM), F32)

    outs = [[] for _ in range(8)]
    for l in range(depth):
        nmw = norm_mix_w[l].reshape(1, d)
        nfw = norm_ffn_w[l].reshape(1, d)
        alog_row = _pad_lanes(a_log[l], HEAD_DIM)
        dtb_row = _pad_lanes(dt_bias[l], HEAD_DIM)
        gnw = gdn_norm_w[l].reshape(1, HEAD_DIM)
        last = l == depth - 1

        proj, ab = _inproj(hp, nmw, w_in_bf[l], w_ab_bf[l], bm_p)
        proj3 = proj.reshape(bp, tp, main_cols)
        oa = _sb_prompt(proj3, sb_bias[l], u_prompt, sb_norm_w[l])
        ob, s_new = _gdn(proj3, ab.reshape(bp, tp, HEAD_DIM), conv0_p, conv_w8[l], alog_row, dtb_row, gnw,
                         s0_p, ltri, lvl_masks, tp)
        hp = _outproj(hp, oa.reshape(bp * tp, gw), ob.reshape(bp * tp, gw), w_out_bf[l], bm_p)
        hp = _ffn(hp, nfw, w_gate_bf[l], w_up_bf[l], w_down_bf[l], final_w, bm_p, 512, last)
        outs[0].append(proj3[:, :, gw:2 * gw].reshape(bp, tp, N_HEADS, HEAD_DIM))
        outs[1].append(proj3[:, :, 2 * gw:3 * gw].reshape(bp, tp, N_HEADS, HEAD_DIM))
        outs[4].append(s_new)
        outs[6].append(proj3[:, tp - (CONV_TAPS - 1):, 3 * gw:6 * gw])

        proj, ab = _inproj(hs, nmw, w_in_bf[l], w_ab_bf[l], bm_s)
        proj3 = proj.reshape(bs, ts, main_cols)
        k_new = proj3[:, :, gw:2 * gw]
        v_new = proj3[:, :, 2 * gw:3 * gw]
        q4 = proj3[:, :, :gw].reshape(bs, ts, N_HEADS, HEAD_DIM) * SB_QUERY_SCALE
        q_rows = jnp.einsum('bthd,hg->bhtgd', q4, head_eye).reshape(bs, N_HEADS * ts, gw).astype(BF16)
        bias_rows = jnp.broadcast_to(jnp.repeat(sb_bias[l] * -LOG2_E, ts)[:, None], (N_HEADS * ts, PAGE_SIZE))
        norm_rows = jnp.repeat(sb_norm_w[l], ts, axis=0)
        pad_keys = ((0, 0), (0, PAGE_SIZE - ts), (0, 0))
        oa = _sb_sample(page_table, q_rows, bias_rows, jnp.pad(k_new, pad_keys), jnp.pad(v_new, pad_keys),
                        cache_k, cache_v, l, u_page, norm_rows)
        oa = oa.reshape(bs, N_HEADS, ts, HEAD_DIM).transpose(0, 2, 1, 3).reshape(bs * ts, gw)
        pad_chunk = ((0, 0), (0, ch - ts), (0, 0))
        conv0_s = jnp.pad(state_conv[l], ((0, 0), (HALO - (CONV_TAPS - 1), 0), (0, 0)))
        ob, s_new = _gdn(jnp.pad(proj3, pad_chunk), jnp.pad(ab.reshape(bs, ts, HEAD_DIM), pad_chunk), conv0_s,
                         conv_w8[l], alog_row, dtb_row, gnw, state_S[l], ltri, lvl_masks, ts)
        hs = _outproj(hs, oa, ob[:, :ts].reshape(bs * ts, gw), w_out_bf[l], bm_s)
        hs = _ffn(hs, nfw, w_gate_bf[l], w_up_bf[l], w_down_bf[l], final_w, bm_s, 512, last)
        outs[2].append(k_new.reshape(bs, ts, N_HEADS, HEAD_DIM))
        outs[3].append(v_new.reshape(bs, ts, N_HEADS, HEAD_DIM))
        outs[5].append(s_new)
        xc_tail = jnp.concatenate([state_conv[l], proj3[:, :, 3 * gw:6 * gw]], axis=1)
        outs[7].append(xc_tail[:, xc_tail.shape[1] - (CONV_TAPS - 1):])

    return (hp.reshape(bp, tp, d), hs.reshape(bs, ts, d)) + tuple(jnp.stack(o) for o in outs)
```

```python
import functools

import jax
import jax.numpy as jnp
from jax import lax
from jax.experimental import pallas as pl
from jax.experimental.pallas import tpu as pltpu

F32 = jnp.float32
BF16 = jnp.bfloat16

NORM_EPS = 1e-6
HEAD_DIM = 128
LOG2_E = 1.4426950408889634
SB_QUERY_SCALE = -LOG2_E * HEAD_DIM ** -0.5
N_HEADS = 8
GROUP_WIDTH = N_HEADS * HEAD_DIM
CONV_TAPS = 4
GDN_CHUNK = 64
INV_LEVELS = 6
PAGE_SIZE = 128
SB_BLOCK = 256
SAMPLE_PAGES_PER_STEP = 8
QUAD = 4
DENSE_ROW_BLOCK = 512
FFN_COL_BLOCK = 512
GDN_SEQS_PER_STEP = 4
HALO = 8
N_MAIN_BLOCKS = 7
K_BLOCK, V_BLOCK = 1, 2
VMEM_LIMIT_BYTES = 48 * 1024 * 1024


def _cparams(*semantics):
    return pltpu.CompilerParams(dimension_semantics=semantics, vmem_limit_bytes=VMEM_LIMIT_BYTES)


def _dot(a, b):
    return jnp.dot(a, b, preferred_element_type=F32)


def _dot_nt(a, b):
    return lax.dot_general(a, b, (((1,), (1,)), ((), ())), preferred_element_type=F32)


def _dot_tn(a, b):
    return lax.dot_general(a, b, (((0,), (0,)), ((), ())), preferred_element_type=F32)


def _split2(x):
    hi = x.astype(BF16)
    lo = (x - hi.astype(F32)).astype(BF16)
    return hi, lo


def _split3(x):
    hi = x.astype(BF16)
    r = x - hi.astype(F32)
    mid = r.astype(BF16)
    lo = (r - mid.astype(F32)).astype(BF16)
    return hi, mid, lo


def _lane_block_masks(rows, n, w):
    lane = lax.broadcasted_iota(jnp.int32, (rows, n * w), 1)
    return [jnp.where((lane >= j * w) & (lane < (j + 1) * w), 1.0, 0.0).astype(BF16) for j in range(n)]


def _lane_block_diag(x, masks):
    return jnp.concatenate([x * m for m in masks], axis=0)


def _mm3_block_diag(x, y, masks):
    yh, yl = _split2(y)
    xh, xl = _split2(x)
    m = x.shape[0]
    both = _dot(jnp.concatenate([xh, xl], axis=0), _lane_block_diag(yh, masks))
    return both[:m] + (_dot(xh, _lane_block_diag(yl, masks)) + both[m:])


def _sigmoid(x):
    return 1.0 / (1.0 + jnp.exp(-x))


def _softplus(x):
    return jnp.maximum(x, 0.0) + jnp.log1p(jnp.exp(-jnp.abs(x)))


def _rms_scale(x):
    return x * lax.rsqrt(jnp.mean(x * x, axis=-1, keepdims=True) + NORM_EPS)


def _inproj_kernel(x_ref, nw_ref, w_ref, wab_ref, *rest, bm, carried):
    proj_ref, ab_ref, kout_ref, vout_ref, xn_ref = rest[2:] if carried else rest
    j = pl.program_id(1)

    @pl.when(j == 0)
    def _():
        xn = (_rms_scale(x_ref[...]) * nw_ref[...]).astype(BF16)
        xn_ref[...] = xn
        ab_ref[...] = _dot(xn, wab_ref[...])

    tile = _dot(xn_ref[...], w_ref[...])
    proj_ref[...] = tile

    def store_head_rows(out_ref):
        for h in range(N_HEADS):
            out_ref[pl.ds(h, bm, stride=N_HEADS), :] = tile[:, h * HEAD_DIM:(h + 1) * HEAD_DIM]

    @pl.when(j == K_BLOCK)
    def _():
        store_head_rows(kout_ref)

    @pl.when(j == V_BLOCK)
    def _():
        store_head_rows(vout_ref)


def _inproj(x, norm_w, w_main, w_ab, bm, layer, depth, k_rows, v_rows):
    m, d = x.shape
    carried = k_rows is not None
    rows_shape = jax.ShapeDtypeStruct((depth, m * N_HEADS, HEAD_DIM), F32)
    rows_spec = pl.BlockSpec((None, bm * N_HEADS, HEAD_DIM), lambda i, j: (layer, i, 0))
    any_spec = pl.BlockSpec(memory_space=pl.ANY)
    return pl.pallas_call(
        functools.partial(_inproj_kernel, bm=bm, carried=carried),
        out_shape=(jax.ShapeDtypeStruct((m, N_MAIN_BLOCKS * GROUP_WIDTH), F32),
                   jax.ShapeDtypeStruct((m, HEAD_DIM), F32), rows_shape, rows_shape),
        grid=(m // bm, N_MAIN_BLOCKS),
        in_specs=[pl.BlockSpec((bm, d), lambda i, j: (i, 0)),
                  pl.BlockSpec((1, d), lambda i, j: (0, 0)),
                  pl.BlockSpec((None, d, GROUP_WIDTH), lambda i, j: (layer, 0, j)),
                  pl.BlockSpec((None, d, HEAD_DIM), lambda i, j: (layer, 0, 0))]
                 + ([any_spec, any_spec] if carried else []),
        out_specs=(pl.BlockSpec((bm, GROUP_WIDTH), lambda i, j: (i, j)),
                   pl.BlockSpec((bm, HEAD_DIM), lambda i, j: (i, 0)), rows_spec, rows_spec),
        scratch_shapes=[pltpu.VMEM((bm, d), BF16)],
        input_output_aliases={4: 2, 5: 3} if carried else {},
        compiler_params=_cparams("parallel", "arbitrary"),
        name="inproj",
    )(x, norm_w, w_main, w_ab, *((k_rows, v_rows) if carried else ()))


def _sb_block(q_neg, kb, vb, u_mat, bias_neg, carry, mask):
    n = _dot_nt(q_neg, kb) + bias_neg
    log_not = jnp.minimum(n, 0.0) - jnp.log2(1.0 + jnp.exp2(-jnp.abs(n)))
    log_beta = log_not - n
    if mask is not None:
        log_not = jnp.where(mask, log_not, 0.0)
    after = _dot(log_not.astype(BF16), u_mat) + carry
    w = jnp.exp2(log_beta + after)
    if mask is not None:
        w = jnp.where(mask, w, 0.0)
    pv = _dot(w.astype(BF16), vb)
    return pv, carry + jnp.sum(log_not, axis=-1, keepdims=True)


def _sb_prompt_kernel(bias_ref, q_ref, k_ref, v_ref, u_ref, nw_ref, o_ref, *, blk):
    h = pl.program_id(1)
    i = pl.program_id(2)
    bias = bias_ref[h] * -LOG2_E
    q0 = (q_ref[0:blk, :] * SB_QUERY_SCALE).astype(BF16)
    q1 = (q_ref[blk:2 * blk, :] * SB_QUERY_SCALE).astype(BF16)
    u_mat = u_ref[...]
    row = lax.broadcasted_iota(jnp.int32, (blk, blk), 0)
    col = lax.broadcasted_iota(jnp.int32, (blk, blk), 1)

    def kv_block(j):
        start = pl.multiple_of(j * blk, blk)
        return k_ref[pl.ds(start, blk), :].astype(BF16), v_ref[pl.ds(start, blk), :].astype(BF16)

    mask = col < row
    zero = jnp.zeros((blk, 1), F32)
    kb, vb = kv_block(2 * i + 1)
    acc1, carry1 = _sb_block(q1, kb, vb, u_mat, bias, zero, mask)
    kb, vb = kv_block(2 * i)
    acc0, carry0 = _sb_block(q0, kb, vb, u_mat, bias, zero, mask)
    pv, carry1 = _sb_block(q1, kb, vb, u_mat, bias, carry1, None)
    acc1 = acc1 + pv

    def body(n, state):
        acc0, carry0, acc1, carry1 = state
        kb, vb = kv_block(2 * i - 1 - n)
        pv0, carry0 = _sb_block(q0, kb, vb, u_mat, bias, carry0, None)
        pv1, carry1 = _sb_block(q1, kb, vb, u_mat, bias, carry1, None)
        return acc0 + pv0, carry0, acc1 + pv1, carry1

    acc0, _, acc1, _ = lax.fori_loop(0, 2 * i, body, (acc0, carry0, acc1, carry1))
    norm_w = nw_ref[pl.ds(h, 1), :]
    o_ref[0:blk, :] = (_rms_scale(acc0) * norm_w).astype(o_ref.dtype)
    o_ref[blk:2 * blk, :] = (_rms_scale(acc1) * norm_w).astype(o_ref.dtype)


def _sb_prompt(proj3, sb_bias, u_mat, norm_w):
    b, t, _ = proj3.shape
    blk = SB_BLOCK
    return pl.pallas_call(
        functools.partial(_sb_prompt_kernel, blk=blk),
        out_shape=jax.ShapeDtypeStruct((b, t, GROUP_WIDTH), BF16),
        grid=(b, N_HEADS, t // (2 * blk)),
        in_specs=[pl.BlockSpec(memory_space=pltpu.SMEM),
                  pl.BlockSpec((None, 2 * blk, HEAD_DIM), lambda bi, h, i: (bi, i, h)),
                  pl.BlockSpec((None, t, HEAD_DIM), lambda bi, h, i: (bi, 0, N_HEADS + h)),
                  pl.BlockSpec((None, t, HEAD_DIM), lambda bi, h, i: (bi, 0, 2 * N_HEADS + h)),
                  pl.BlockSpec((blk, blk), lambda bi, h, i: (0, 0)),
                  pl.BlockSpec((N_HEADS, HEAD_DIM), lambda bi, h, i: (0, 0))],
        out_specs=pl.BlockSpec((None, 2 * blk, HEAD_DIM), lambda bi, h, i: (bi, i, h)),
        compiler_params=_cparams("parallel", "parallel", "arbitrary"),
        name="sb_prompt",
    )(sb_bias, proj3, proj3, proj3, u_mat, norm_w)


def _page_rows(page_ref):
    return jnp.concatenate([page_ref[pl.ds(h, PAGE_SIZE, stride=N_HEADS), :] for h in range(N_HEADS)],
                           axis=1).astype(BF16)


def _sb_sample_kernel(pt_ref, q_ref, bias_ref, knew_ref, vnew_ref, *rest, n_steps, t_new, group):
    del pt_ref
    k_refs, v_refs = rest[:group], rest[group:2 * group]
    u_ref, nw_ref, o_ref, acc_ref, carry_ref = rest[2 * group:]
    p = pl.program_id(1)
    rows = N_HEADS * t_new
    q = q_ref[...]
    bias = bias_ref[...]
    u_mat = u_ref[...]
    row = lax.broadcasted_iota(jnp.int32, (rows, PAGE_SIZE), 0)
    col = lax.broadcasted_iota(jnp.int32, (rows, PAGE_SIZE), 1)

    @pl.when(p == 0)
    def _():
        pv, carry = _sb_block(q, knew_ref[...].astype(BF16), vnew_ref[...].astype(BF16), u_mat, bias,
                              jnp.zeros((rows, 1), F32), col < lax.rem(row, t_new))
        acc_ref[...] = pv
        carry_ref[...] = jnp.broadcast_to(carry, carry_ref.shape)

    @pl.when(p > 0)
    def _():
        carry = carry_ref[:, 0:1]
        acc = acc_ref[...]
        for g in range(group):
            pv, carry = _sb_block(q, _page_rows(k_refs[g]), _page_rows(v_refs[g]), u_mat, bias, carry, None)
            acc = acc + pv
        acc_ref[...] = acc
        carry_ref[...] = jnp.broadcast_to(carry, carry_ref.shape)

    @pl.when(p == n_steps)
    def _():
        o = jnp.zeros((rows, HEAD_DIM), F32)
        for h in range(N_HEADS):
            in_head = (row >= h * t_new) & (row < (h + 1) * t_new)
            o = o + jnp.where(in_head, acc_ref[:, h * HEAD_DIM:(h + 1) * HEAD_DIM], 0.0)
        o_ref[...] = (_rms_scale(o) * nw_ref[...]).astype(o_ref.dtype)


def _sb_sample(page_table, q_rows, bias_rows, k_new, v_new, cache_k, cache_v, layer, u_mat, norm_rows):
    b, rows, _ = q_rows.shape
    n_pages = page_table.shape[1]
    t_new = rows // N_HEADS
    group = max(g for g in range(1, SAMPLE_PAGES_PER_STEP + 1) if n_pages % g == 0)
    n_steps = n_pages // group

    def page_spec(g):
        def page_map(bi, p, pt):
            return (layer, pt[bi, n_pages - 1 - (jnp.maximum(p - 1, 0) * group + g)], 0, 0)
        return pl.BlockSpec((None, None, PAGE_SIZE * N_HEADS, HEAD_DIM), page_map)

    page_specs = [page_spec(g) for g in range(group)]
    grid_spec = pltpu.PrefetchScalarGridSpec(
        num_scalar_prefetch=1,
        grid=(b, n_steps + 1),
        in_specs=[pl.BlockSpec((None, rows, GROUP_WIDTH), lambda bi, p, pt: (bi, 0, 0)),
                  pl.BlockSpec((rows, PAGE_SIZE), lambda bi, p, pt: (0, 0)),
                  pl.BlockSpec((None, PAGE_SIZE, GROUP_WIDTH), lambda bi, p, pt: (bi, 0, 0)),
                  pl.BlockSpec((None, PAGE_SIZE, GROUP_WIDTH), lambda bi, p, pt: (bi, 0, 0))]
                 + page_specs + page_specs
                 + [pl.BlockSpec((PAGE_SIZE, PAGE_SIZE), lambda bi, p, pt: (0, 0)),
                    pl.BlockSpec((rows, HEAD_DIM), lambda bi, p, pt: (0, 0))],
        out_specs=pl.BlockSpec((None, rows, HEAD_DIM), lambda bi, p, pt: (bi, 0, 0)),
        scratch_shapes=[pltpu.VMEM((rows, GROUP_WIDTH), F32), pltpu.VMEM((rows, PAGE_SIZE), F32)],
    )
    return pl.pallas_call(
        functools.partial(_sb_sample_kernel, n_steps=n_steps, t_new=t_new, group=group),
        out_shape=jax.ShapeDtypeStruct((b, rows, HEAD_DIM), BF16),
        grid_spec=grid_spec,
        compiler_params=_cparams("parallel", "arbitrary"),
        name="sb_sample",
    )(page_table, q_rows, bias_rows, k_new, v_new, *([cache_k] * group), *([cache_v] * group), u_mat, norm_rows)


def _gdn_kernel(xq_ref, xk_ref, xv_ref, z_ref, ab_ref, c0q_ref, c0k_ref, c0v_ref, cwq_ref, cwk_ref, cwv_ref,
                alog_ref, dtb_ref, nw_ref, s0_ref, ltri_ref, lvl_ref, ob_ref, sfin_ref,
                s_scr, bq_scr, bk_scr, bv_scr, *, t_valid, n_chunks, n_seq):
    c = pl.program_id(1)
    ch = GDN_CHUNK
    seqs = range(n_seq)
    heads = range(n_seq * N_HEADS)
    quads = range(n_seq * N_HEADS // QUAD)

    @pl.when(c == 0)
    def _():
        for n in seqs:
            for h in range(N_HEADS):
                s_scr[n * N_HEADS + h] = s0_ref[n, h]
            bq_scr[n, 0:HALO, :] = c0q_ref[n]
            bk_scr[n, 0:HALO, :] = c0k_ref[n]
            bv_scr[n, 0:HALO, :] = c0v_ref[n]

    def conv_silu(x_ref, buf, cw_ref, n):
        buf[n, HALO:HALO + ch, :] = x_ref[n]
        x = buf[n]
        y = pltpu.roll(x, CONV_TAPS - 1, axis=0)[HALO:, :] * cw_ref[0:1, :]
        for tap in range(1, CONV_TAPS - 1):
            y = y + pltpu.roll(x, CONV_TAPS - 1 - tap, axis=0)[HALO:, :] * cw_ref[tap:tap + 1, :]
        y = y + x[HALO:, :] * cw_ref[CONV_TAPS - 1:CONV_TAPS, :]
        buf[n, 0:HALO, :] = x[ch:ch + HALO, :]
        return y * _sigmoid(y)

    yq = jnp.concatenate([conv_silu(xq_ref, bq_scr, cwq_ref, n) for n in seqs], axis=1)
    yk = jnp.concatenate([conv_silu(xk_ref, bk_scr, cwk_ref, n) for n in seqs], axis=1)
    yv = jnp.concatenate([conv_silu(xv_ref, bv_scr, cwv_ref, n) for n in seqs], axis=1)

    ltri = ltri_ref[...]
    tok = c * ch + lax.broadcasted_iota(jnp.int32, (ch, HEAD_DIM), 0)
    valid = tok < t_valid
    beta, gam_c, e_gam, e_rest, g_tot = [], [], [], [], []
    for n in seqs:
        ab = ab_ref[n]
        log_g = jnp.where(valid, -jnp.exp(alog_ref[...]) * _softplus(ab + dtb_ref[...]), 0.0)
        beta.append(jnp.where(valid, _sigmoid(ab), 0.0))
        g_hi, g_mid, g_lo = _split3(log_g)
        gam = _dot(ltri, g_hi) + (_dot(ltri, g_mid) + _dot(ltri, g_lo))
        gam_last = gam[ch - 1:ch, :]
        gam_c.append(gam)
        e_gam.append(jnp.exp(gam))
        e_rest.append(jnp.exp(gam_last - gam))
        g_tot.append(jnp.exp(gam_last))

    hsl = [slice(h * HEAD_DIM, (h + 1) * HEAD_DIM) for h in heads]
    qsl = [slice(g * QUAD * HEAD_DIM, (g + 1) * QUAD * HEAD_DIM) for g in quads]
    lsl = [slice((h % QUAD) * HEAD_DIM, (h % QUAD + 1) * HEAD_DIM) for h in heads]

    def head_col(mats, h, first_col=0):
        col = first_col + h % N_HEADS
        return mats[h // N_HEADS][:, col:col + 1]

    def per_head(mats, first_col):
        return jnp.concatenate([jnp.broadcast_to(head_col(mats, h, first_col), (ch, HEAD_DIM)) for h in heads], axis=1)

    def l2_normalize(y):
        return jnp.concatenate([y[:, sl] * lax.rsqrt(jnp.sum(y[:, sl] * y[:, sl], axis=-1, keepdims=True) + NORM_EPS)
                                for sl in hsl], axis=1)

    beta_x = per_head(beta, N_HEADS)
    e_gam_x = per_head(e_gam, 0)
    q = l2_normalize(yq) * HEAD_DIM ** -0.5
    k = l2_normalize(yk)
    k_beta = k * beta_x
    v_beta = yv * beta_x
    k_beta_gam = k_beta * e_gam_x
    q_gam = (q * e_gam_x).astype(BF16)
    k_rest = (k * per_head(e_rest, 0)).astype(BF16)

    qw = QUAD * ch
    ri = lax.broadcasted_iota(jnp.int32, (ch, qw), 0)
    li = lax.broadcasted_iota(jnp.int32, (ch, qw), 1)
    ci = li & (ch - 1)
    incl = ri >= ci
    strict = ri > ci
    eye = jnp.where(ri == ci, 1.0, 0.0).astype(F32)
    masks_mat = _lane_block_masks(ch, QUAD, ch)
    masks_vec = _lane_block_masks(ch, QUAD, HEAD_DIM)

    a_mat, attn, inv = [], [], []
    for g in quads:
        gam_q = jnp.broadcast_to(head_col(gam_c, QUAD * g + QUAD - 1), (ch, qw))
        for j in range(QUAD - 2, -1, -1):
            gam_q = jnp.where(li < (j + 1) * ch, jnp.broadcast_to(head_col(gam_c, QUAD * g + j), (ch, qw)), gam_q)
        diff = gam_q - jnp.sum(gam_q * eye, axis=0, keepdims=True)
        decay = jnp.where(incl, jnp.exp(jnp.where(incl, diff, 0.0)), 0.0)
        k_diag = _lane_block_diag(k[:, qsl[g]].astype(BF16), masks_vec)
        a_mat.append(jnp.where(strict, _dot_nt(k_beta[:, qsl[g]].astype(BF16), k_diag) * decay, 0.0))
        attn.append((_dot_nt(q[:, qsl[g]].astype(BF16), k_diag) * decay).astype(BF16))
        inv.append(eye - a_mat[g] * lvl_ref[0])
    for lvl in range(1, INV_LEVELS):
        inner = [_mm3_block_diag(a_mat[g] * lvl_ref[lvl], inv[g], masks_mat) for g in quads]
        inv = [inv[g] - _mm3_block_diag(inv[g], inner[g], masks_mat) for g in quads]
    u = [_mm3_block_diag(inv[g], v_beta[:, qsl[g]], masks_vec) for g in quads]
    w = [_mm3_block_diag(inv[g], k_beta_gam[:, qsl[g]], masks_vec) for g in quads]
    s_old = [s_scr[h] for h in heads]
    s_bf = [s.astype(BF16) for s in s_old]
    v_new = [u[h // QUAD][:, lsl[h]] - _dot(w[h // QUAD][:, lsl[h]].astype(BF16), s_bf[h]) for h in heads]
    o_state = [_dot(q_gam[:, hsl[h]], s_bf[h]) for h in heads]
    v_new_bf = [v.astype(BF16) for v in v_new]
    for h in heads:
        s_scr[h] = s_old[h] * head_col(g_tot, h) + _dot_tn(k_rest[:, hsl[h]], v_new_bf[h])
    o_chunk = [_dot(attn[g], _lane_block_diag(jnp.concatenate(v_new_bf[QUAD * g:QUAD * (g + 1)], axis=1), masks_vec))
               for g in quads]
    for h in heads:
        o = o_state[h] + o_chunk[h // QUAD][:, lsl[h]]
        n, sl = h // N_HEADS, hsl[h % N_HEADS]
        z = z_ref[n, :, sl]
        ob_ref[n, :, sl] = (_rms_scale(o) * nw_ref[...] * (z * _sigmoid(z))).astype(ob_ref.dtype)

    @pl.when(c == n_chunks - 1)
    def _():
        for n in seqs:
            for h in range(N_HEADS):
                sfin_ref[n, h] = s_scr[n * N_HEADS + h]


def _gdn(proj3, ab3, conv0, conv_w, alog_row, dtb_row, norm_w, s0, ltri, lvl_masks, t_valid):
    b, t, _ = proj3.shape
    ch = GDN_CHUNK
    n_chunks = t // ch
    gw = GROUP_WIDTH
    n_seq = GDN_SEQS_PER_STEP if b % GDN_SEQS_PER_STEP == 0 else 1

    def col_spec(rows, blk_col):
        return pl.BlockSpec((n_seq, rows, gw), lambda bi, c: (bi, c if rows == ch else 0, blk_col))

    def cw_spec(blk_col):
        return pl.BlockSpec((HALO, gw), lambda bi, c: (0, blk_col))

    row_spec = pl.BlockSpec((1, HEAD_DIM), lambda bi, c: (0, 0))
    state_spec = pl.BlockSpec((n_seq, N_HEADS, HEAD_DIM, HEAD_DIM), lambda bi, c: (bi, 0, 0, 0))
    return pl.pallas_call(
        functools.partial(_gdn_kernel, t_valid=t_valid, n_chunks=n_chunks, n_seq=n_seq),
        out_shape=(jax.ShapeDtypeStruct((b, t, gw), BF16),
                   jax.ShapeDtypeStruct((b, N_HEADS, HEAD_DIM, HEAD_DIM), F32)),
        grid=(b // n_seq, n_chunks),
        in_specs=[col_spec(ch, 3), col_spec(ch, 4), col_spec(ch, 5), col_spec(ch, 6),
                  pl.BlockSpec((n_seq, ch, HEAD_DIM), lambda bi, c: (bi, c, 0)),
                  col_spec(HALO, 0), col_spec(HALO, 1), col_spec(HALO, 2),
                  cw_spec(0), cw_spec(1), cw_spec(2),
                  row_spec, row_spec, row_spec, state_spec,
                  pl.BlockSpec((ch, ch), lambda bi, c: (0, 0)),
                  pl.BlockSpec((INV_LEVELS, ch, QUAD * ch), lambda bi, c: (0, 0, 0))],
        out_specs=(pl.BlockSpec((n_seq, ch, gw), lambda bi, c: (bi, c, 0)), state_spec),
        scratch_shapes=[pltpu.VMEM((n_seq * N_HEADS, HEAD_DIM, HEAD_DIM), F32),
                        pltpu.VMEM((n_seq, HALO + ch, gw), F32),
                        pltpu.VMEM((n_seq, HALO + ch, gw), F32),
                        pltpu.VMEM((n_seq, HALO + ch, gw), F32)],
        compiler_params=_cparams("parallel", "arbitrary"),
        name="gdn",
    )(proj3, proj3, proj3, proj3, ab3, conv0, conv0, conv0, conv_w, conv_w, conv_w,
      alog_row, dtb_row, norm_w, s0, ltri, lvl_masks)


def _outproj_kernel(h_ref, oa_ref, ob_ref, wa_ref, wb_ref, o_ref):
    o_ref[...] = h_ref[...] + (_dot(oa_ref[...], wa_ref[...]) + _dot(ob_ref[...], wb_ref[...]))


def _outproj(h, oa, ob, w_out, layer, bm):
    m, d = h.shape
    gw = GROUP_WIDTH
    return pl.pallas_call(
        _outproj_kernel,
        out_shape=jax.ShapeDtypeStruct((m, d), F32),
        grid=(m // bm,),
        in_specs=[pl.BlockSpec((bm, d), lambda i: (i, 0)),
                  pl.BlockSpec((bm, gw), lambda i: (i, 0)),
                  pl.BlockSpec((bm, gw), lambda i: (i, 0)),
                  pl.BlockSpec((None, gw, d), lambda i: (layer, 0, 0)),
                  pl.BlockSpec((None, gw, d), lambda i: (layer, 1, 0))],
        out_specs=pl.BlockSpec((bm, d), lambda i: (i, 0)),
        compiler_params=_cparams("parallel"),
        name="outproj",
    )(h, oa, ob, w_out, w_out)


def _ffn_kernel(h_ref, nw_ref, wg_ref, wu_ref, wd_ref, fnw_ref, o_ref, xn_ref, acc_ref, *, n_f, final_norm):
    f = pl.program_id(1)

    @pl.when(f == 0)
    def _():
        xn_ref[...] = (_rms_scale(h_ref[...]) * nw_ref[...]).astype(BF16)
        acc_ref[...] = jnp.zeros_like(acc_ref)

    xn = xn_ref[...]
    g = _dot(xn, wg_ref[...])
    act = (g * _sigmoid(g)) * _dot(xn, wu_ref[...])
    acc_ref[...] += _dot(act.astype(BF16), wd_ref[...])

    @pl.when(f == n_f - 1)
    def _():
        out = h_ref[...] + acc_ref[...]
        if final_norm:
            out = _rms_scale(out) * fnw_ref[...]
        o_ref[...] = out


def _ffn(h, norm_w, w_gate, w_up, w_down, final_w, layer, bm, bf, final_norm):
    m, d = h.shape
    f_dim = w_gate.shape[2]
    n_f = f_dim // bf
    return pl.pallas_call(
        functools.partial(_ffn_kernel, n_f=n_f, final_norm=final_norm),
        out_shape=jax.ShapeDtypeStruct((m, d), F32),
        grid=(m // bm, n_f),
        in_specs=[pl.BlockSpec((bm, d), lambda i, f: (i, 0)),
                  pl.BlockSpec((1, d), lambda i, f: (0, 0)),
                  pl.BlockSpec((None, d, bf), lambda i, f: (layer, 0, f)),
                  pl.BlockSpec((None, d, bf), lambda i, f: (layer, 0, f)),
                  pl.BlockSpec((None, bf, d), lambda i, f: (layer, f, 0)),
                  pl.BlockSpec((1, d), lambda i, f: (0, 0))],
        out_specs=pl.BlockSpec((bm, d), lambda i, f: (i, 0)),
        scratch_shapes=[pltpu.VMEM((bm, d), BF16), pltpu.VMEM((bm, d), F32)],
        compiler_params=_cparams("parallel", "arbitrary"),
        name="ffn",
    )(h, norm_w, w_gate, w_up, w_down, final_w)


def _row_block(m, target):
    return target if m % target == 0 else m


def _excl_upper(n):
    j = lax.broadcasted_iota(jnp.int32, (n, n), 0)
    s = lax.broadcasted_iota(jnp.int32, (n, n), 1)
    return (j > s).astype(BF16)


def _pad_lanes(row, width):
    return jnp.pad(row, (0, width - row.shape[0])).reshape(1, width)


def kernel(x_prompt, x_sample, cache_k, cache_v, state_S, state_conv, page_table, norm_mix_w, w_in, conv_w, a_log, dt_bias, sb_bias, sb_norm_w, gdn_norm_w, w_out, norm_ffn_w, w_gate, w_up, w_down, final_norm_w):
    bp, tp, d = x_prompt.shape
    bs, ts, _ = x_sample.shape
    depth = w_in.shape[0]
    gw = GROUP_WIDTH
    main_cols = N_MAIN_BLOCKS * gw
    qkv_cols = 3 * gw
    ch = GDN_CHUNK
    assert w_in.shape[2] == main_cols + 2 * N_HEADS and tp % (2 * SB_BLOCK) == 0 and ts <= ch
    assert cache_k.shape[2:] == (PAGE_SIZE, N_HEADS, HEAD_DIM)

    w_in_bf = w_in.astype(BF16)
    w_ab_bf = jnp.pad(w_in[:, :, main_cols:].astype(BF16), ((0, 0), (0, 0), (0, HEAD_DIM - 2 * N_HEADS)))
    w_out_bf = w_out.astype(BF16)
    w_gate_bf = w_gate.astype(BF16)
    w_up_bf = w_up.astype(BF16)
    w_down_bf = w_down.astype(BF16)
    conv_w8 = jnp.pad(conv_w, ((0, 0), (0, HALO - CONV_TAPS), (0, 0)))
    cache_k = cache_k.reshape(cache_k.shape[:2] + (PAGE_SIZE * N_HEADS, HEAD_DIM))
    cache_v = cache_v.reshape(cache_v.shape[:2] + (PAGE_SIZE * N_HEADS, HEAD_DIM))

    u_prompt = _excl_upper(SB_BLOCK)
    u_page = _excl_upper(PAGE_SIZE)
    ri = lax.broadcasted_iota(jnp.int32, (ch, ch), 0)
    ci = lax.broadcasted_iota(jnp.int32, (ch, ch), 1)
    ltri = (ri >= ci).astype(BF16)
    lvl_masks = jnp.stack([((ri // (2 * b) == ci // (2 * b)) & ((ri & b) != 0) & ((ci & b) == 0)).astype(F32)
                           for b in (1 << e for e in range(INV_LEVELS))])
    lvl_masks = jnp.tile(lvl_masks, (1, 1, QUAD))
    head_eye = jnp.eye(N_HEADS, dtype=F32)
    final_w = final_norm_w.reshape(1, d)

    hp = x_prompt.reshape(bp * tp, d)
    hs = x_sample.reshape(bs * ts, d)
    bm_p = _row_block(bp * tp, DENSE_ROW_BLOCK)
    bm_s = bs * ts
    conv0_p = jnp.zeros((bp, HALO, qkv_cols), F32)
    s0_p = jnp.zeros((bp, N_HEADS, HEAD_DIM, HEAD_DIM), F32)

    outs = [[] for _ in range(8)]
    kp_rows = vp_rows = ks_rows = vs_rows = None
    for l in range(depth):
        nmw = norm_mix_w[l].reshape(1, d)
        nfw = norm_ffn_w[l].reshape(1, d)
        alog_row = _pad_lanes(a_log[l], HEAD_DIM)
        dtb_row = _pad_lanes(dt_bias[l], HEAD_DIM)
        gnw = gdn_norm_w[l].reshape(1, HEAD_DIM)
        last = l == depth - 1

        proj, ab, kp_rows, vp_rows = _inproj(hp, nmw, w_in_bf, w_ab_bf, bm_p, l, depth, kp_rows, vp_rows)
        proj3 = proj.reshape(bp, tp, main_cols)
        oa = _sb_prompt(proj3, sb_bias[l], u_prompt, sb_norm_w[l])
        ob, s_new = _gdn(proj3, ab.reshape(bp, tp, HEAD_DIM), conv0_p, conv_w8[l], alog_row, dtb_row, gnw,
                         s0_p, ltri, lvl_masks, tp)
        hp = _outproj(hp, oa.reshape(bp * tp, gw), ob.reshape(bp * tp, gw), w_out_bf, l, bm_p)
        hp = _ffn(hp, nfw, w_gate_bf, w_up_bf, w_down_bf, final_w, l, bm_p, FFN_COL_BLOCK, last)
        outs[4].append(s_new)
        outs[6].append(proj3[:, tp - (CONV_TAPS - 1):, 3 * gw:6 * gw])

        proj, ab, ks_rows, vs_rows = _inproj(hs, nmw, w_in_bf, w_ab_bf, bm_s, l, depth, ks_rows, vs_rows)
        proj3 = proj.reshape(bs, ts, main_cols)
        k_new = proj3[:, :, gw:2 * gw]
        v_new = proj3[:, :, 2 * gw:3 * gw]
        q4 = proj3[:, :, :gw].reshape(bs, ts, N_HEADS, HEAD_DIM) * SB_QUERY_SCALE
        q_rows = jnp.einsum('bthd,hg->bhtgd', q4, head_eye).reshape(bs, N_HEADS * ts, gw).astype(BF16)
        bias_rows = jnp.broadcast_to(jnp.repeat(sb_bias[l] * -LOG2_E, ts)[:, None], (N_HEADS * ts, PAGE_SIZE))
        norm_rows = jnp.repeat(sb_norm_w[l], ts, axis=0)
        pad_keys = ((0, 0), (0, PAGE_SIZE - ts), (0, 0))
        oa = _sb_sample(page_table, q_rows, bias_rows, jnp.pad(k_new, pad_keys), jnp.pad(v_new, pad_keys),
                        cache_k, cache_v, l, u_page, norm_rows)
        oa = oa.reshape(bs, N_HEADS, ts, HEAD_DIM).transpose(0, 2, 1, 3).reshape(bs * ts, gw)
        pad_chunk = ((0, 0), (0, ch - ts), (0, 0))
        conv0_s = jnp.pad(state_conv[l], ((0, 0), (HALO - (CONV_TAPS - 1), 0), (0, 0)))
        ob, s_new = _gdn(jnp.pad(proj3, pad_chunk), jnp.pad(ab.reshape(bs, ts, HEAD_DIM), pad_chunk), conv0_s,
                         conv_w8[l], alog_row, dtb_row, gnw, state_S[l], ltri, lvl_masks, ts)
        hs = _outproj(hs, oa, ob[:, :ts].reshape(bs * ts, gw), w_out_bf, l, bm_s)
        hs = _ffn(hs, nfw, w_gate_bf, w_up_bf, w_down_bf, final_w, l, bm_s, FFN_COL_BLOCK, last)
        outs[5].append(s_new)
        xc_tail = jnp.concatenate([state_conv[l], proj3[:, :, 3 * gw:6 * gw]], axis=1)
        outs[7].append(xc_tail[:, xc_tail.shape[1] - (CONV_TAPS - 1):])

    outs[0] = kp_rows.reshape(depth, bp, tp, N_HEADS, HEAD_DIM)
    outs[1] = vp_rows.reshape(depth, bp, tp, N_HEADS, HEAD_DIM)
    outs[2] = ks_rows.reshape(depth, bs, ts, N_HEADS, HEAD_DIM)
    outs[3] = vs_rows.reshape(depth, bs, ts, N_HEADS, HEAD_DIM)
    return (hp.reshape(bp, tp, d), hs.reshape(bs, ts, d)) + tuple(o if i < 4 else jnp.stack(o)
                                                                  for i, o in enumerate(outs))
```

```python
import functools

import jax
import jax.numpy as jnp
from jax import lax
from jax.experimental import pallas as pl
from jax.experimental.pallas import tpu as pltpu

F32 = jnp.float32
BF16 = jnp.bfloat16

NORM_EPS = 1e-6
HEAD_DIM = 128
LOG2_E = 1.4426950408889634
SB_QUERY_SCALE = -LOG2_E * HEAD_DIM ** -0.5
N_HEADS = 8
GROUP_WIDTH = N_HEADS * HEAD_DIM
CONV_TAPS = 4
GDN_CHUNK = 64
INV_LEVELS = 6
PAGE_SIZE = 128
SB_BLOCK = 256
SAMPLE_PAGES_PER_STEP = 8
SAMPLE_SEQS_PER_STEP = 2
QUAD = 4
DENSE_ROW_BLOCK = 512
FFN_COL_BLOCK = 512
GDN_SEQS_PER_STEP = 4
HALO = 8
N_MAIN_BLOCKS = 7
K_BLOCK, V_BLOCK = 1, 2
VMEM_LIMIT_BYTES = 48 * 1024 * 1024


def _cparams(*semantics):
    return pltpu.CompilerParams(dimension_semantics=semantics, vmem_limit_bytes=VMEM_LIMIT_BYTES)


def _dot(a, b):
    return jnp.dot(a, b, preferred_element_type=F32)


def _dot_nt(a, b):
    return lax.dot_general(a, b, (((1,), (1,)), ((), ())), preferred_element_type=F32)


def _dot_tn(a, b):
    return lax.dot_general(a, b, (((0,), (0,)), ((), ())), preferred_element_type=F32)


def _split2(x):
    hi = x.astype(BF16)
    lo = (x - hi.astype(F32)).astype(BF16)
    return hi, lo


def _split3(x):
    hi = x.astype(BF16)
    r = x - hi.astype(F32)
    mid = r.astype(BF16)
    lo = (r - mid.astype(F32)).astype(BF16)
    return hi, mid, lo


def _lane_block_masks(rows, n, w):
    lane = lax.broadcasted_iota(jnp.int32, (rows, n * w), 1)
    return [jnp.where((lane >= j * w) & (lane < (j + 1) * w), 1.0, 0.0).astype(BF16) for j in range(n)]


def _lane_block_diag(x, masks):
    return jnp.concatenate([x * m for m in masks], axis=0)


def _mm3_block_diag(x, y, masks):
    yh, yl = _split2(y)
    xh, xl = _split2(x)
    m = x.shape[0]
    both = _dot(jnp.concatenate([xh, xl], axis=0), _lane_block_diag(yh, masks))
    return both[:m] + (_dot(xh, _lane_block_diag(yl, masks)) + both[m:])


def _sigmoid(x):
    return 1.0 / (1.0 + jnp.exp(-x))


def _softplus(x):
    return jnp.maximum(x, 0.0) + jnp.log1p(jnp.exp(-jnp.abs(x)))


def _rms_scale(x):
    return x * lax.rsqrt(jnp.mean(x * x, axis=-1, keepdims=True) + NORM_EPS)


def _inproj_kernel(x_ref, nw_ref, w_ref, wab_ref, *rest, bm, carried):
    proj_ref, ab_ref, kout_ref, vout_ref, xn_ref = rest[2:] if carried else rest
    j = pl.program_id(1)

    @pl.when(j == 0)
    def _():
        xn = (_rms_scale(x_ref[...]) * nw_ref[...]).astype(BF16)
        xn_ref[...] = xn
        ab_ref[...] = _dot(xn, wab_ref[...])

    tile = _dot(xn_ref[...], w_ref[...])
    proj_ref[...] = tile

    def store_head_rows(out_ref):
        for h in range(N_HEADS):
            out_ref[pl.ds(h, bm, stride=N_HEADS), :] = tile[:, h * HEAD_DIM:(h + 1) * HEAD_DIM]

    @pl.when(j == K_BLOCK)
    def _():
        store_head_rows(kout_ref)

    @pl.when(j == V_BLOCK)
    def _():
        store_head_rows(vout_ref)


def _inproj(x, norm_w, w_main, w_ab, bm, layer, depth, k_rows, v_rows):
    m, d = x.shape
    carried = k_rows is not None
    rows_shape = jax.ShapeDtypeStruct((depth, m * N_HEADS, HEAD_DIM), F32)
    rows_spec = pl.BlockSpec((None, bm * N_HEADS, HEAD_DIM), lambda i, j: (layer, i, 0))
    any_spec = pl.BlockSpec(memory_space=pl.ANY)
    return pl.pallas_call(
        functools.partial(_inproj_kernel, bm=bm, carried=carried),
        out_shape=(jax.ShapeDtypeStruct((m, N_MAIN_BLOCKS * GROUP_WIDTH), F32),
                   jax.ShapeDtypeStruct((m, HEAD_DIM), F32), rows_shape, rows_shape),
        grid=(m // bm, N_MAIN_BLOCKS),
        in_specs=[pl.BlockSpec((bm, d), lambda i, j: (i, 0)),
                  pl.BlockSpec((1, d), lambda i, j: (0, 0)),
                  pl.BlockSpec((None, d, GROUP_WIDTH), lambda i, j: (layer, 0, j)),
                  pl.BlockSpec((None, d, HEAD_DIM), lambda i, j: (layer, 0, 0))]
                 + ([any_spec, any_spec] if carried else []),
        out_specs=(pl.BlockSpec((bm, GROUP_WIDTH), lambda i, j: (i, j)),
                   pl.BlockSpec((bm, HEAD_DIM), lambda i, j: (i, 0)), rows_spec, rows_spec),
        scratch_shapes=[pltpu.VMEM((bm, d), BF16)],
        input_output_aliases={4: 2, 5: 3} if carried else {},
        compiler_params=_cparams("parallel", "arbitrary"),
        name="inproj",
    )(x, norm_w, w_main, w_ab, *((k_rows, v_rows) if carried else ()))


def _sb_block(q_neg, kb, vb, u_mat, bias_neg, carry, mask):
    n = _dot_nt(q_neg, kb) + bias_neg
    log_not = jnp.minimum(n, 0.0) - jnp.log2(1.0 + jnp.exp2(-jnp.abs(n)))
    log_beta = log_not - n
    if mask is not None:
        log_not = jnp.where(mask, log_not, 0.0)
    after = _dot(log_not.astype(BF16), u_mat) + carry
    w = jnp.exp2(log_beta + after)
    if mask is not None:
        w = jnp.where(mask, w, 0.0)
    pv = _dot(w.astype(BF16), vb)
    return pv, carry + jnp.sum(log_not, axis=-1, keepdims=True)


def _sb_prompt_kernel(bias_ref, q_ref, k_ref, v_ref, u_ref, nw_ref, o_ref, *, blk):
    h = pl.program_id(1)
    i = pl.program_id(2)
    bias = bias_ref[h] * -LOG2_E
    q0 = (q_ref[0:blk, :] * SB_QUERY_SCALE).astype(BF16)
    q1 = (q_ref[blk:2 * blk, :] * SB_QUERY_SCALE).astype(BF16)
    u_mat = u_ref[...]
    row = lax.broadcasted_iota(jnp.int32, (blk, blk), 0)
    col = lax.broadcasted_iota(jnp.int32, (blk, blk), 1)

    def kv_block(j):
        start = pl.multiple_of(j * blk, blk)
        return k_ref[pl.ds(start, blk), :].astype(BF16), v_ref[pl.ds(start, blk), :].astype(BF16)

    mask = col < row
    zero = jnp.zeros((blk, 1), F32)
    kb, vb = kv_block(2 * i + 1)
    acc1, carry1 = _sb_block(q1, kb, vb, u_mat, bias, zero, mask)
    kb, vb = kv_block(2 * i)
    acc0, carry0 = _sb_block(q0, kb, vb, u_mat, bias, zero, mask)
    pv, carry1 = _sb_block(q1, kb, vb, u_mat, bias, carry1, None)
    acc1 = acc1 + pv

    def body(n, state):
        acc0, carry0, acc1, carry1 = state
        kb, vb = kv_block(2 * i - 1 - n)
        pv0, carry0 = _sb_block(q0, kb, vb, u_mat, bias, carry0, None)
        pv1, carry1 = _sb_block(q1, kb, vb, u_mat, bias, carry1, None)
        return acc0 + pv0, carry0, acc1 + pv1, carry1

    acc0, _, acc1, _ = lax.fori_loop(0, 2 * i, body, (acc0, carry0, acc1, carry1))
    norm_w = nw_ref[pl.ds(h, 1), :]
    o_ref[0:blk, :] = (_rms_scale(acc0) * norm_w).astype(o_ref.dtype)
    o_ref[blk:2 * blk, :] = (_rms_scale(acc1) * norm_w).astype(o_ref.dtype)


def _sb_prompt(proj3, sb_bias, u_mat, norm_w):
    b, t, _ = proj3.shape
    blk = SB_BLOCK
    return pl.pallas_call(
        functools.partial(_sb_prompt_kernel, blk=blk),
        out_shape=jax.ShapeDtypeStruct((b, t, GROUP_WIDTH), BF16),
        grid=(b, N_HEADS, t // (2 * blk)),
        in_specs=[pl.BlockSpec(memory_space=pltpu.SMEM),
                  pl.BlockSpec((None, 2 * blk, HEAD_DIM), lambda bi, h, i: (bi, i, h)),
                  pl.BlockSpec((None, t, HEAD_DIM), lambda bi, h, i: (bi, 0, N_HEADS + h)),
                  pl.BlockSpec((None, t, HEAD_DIM), lambda bi, h, i: (bi, 0, 2 * N_HEADS + h)),
                  pl.BlockSpec((blk, blk), lambda bi, h, i: (0, 0)),
                  pl.BlockSpec((N_HEADS, HEAD_DIM), lambda bi, h, i: (0, 0))],
        out_specs=pl.BlockSpec((None, 2 * blk, HEAD_DIM), lambda bi, h, i: (bi, i, h)),
        compiler_params=_cparams("parallel", "parallel", "arbitrary"),
        name="sb_prompt",
    )(sb_bias, proj3, proj3, proj3, u_mat, norm_w)


def _page_rows(page_ref):
    return jnp.concatenate([page_ref[pl.ds(h, PAGE_SIZE, stride=N_HEADS), :] for h in range(N_HEADS)],
                           axis=1).astype(BF16)


def _sb_sample_kernel(pt_ref, q_ref, bias_ref, knew_ref, vnew_ref, *rest, n_steps, t_new, group, n_seq):
    del pt_ref
    n_pg = n_seq * group
    k_refs, v_refs = rest[:n_pg], rest[n_pg:2 * n_pg]
    u_ref, nw_ref, o_ref, acc_ref, carry_ref = rest[2 * n_pg:]
    p = pl.program_id(1)
    rows = N_HEADS * t_new
    seqs = range(n_seq)
    bias = bias_ref[...]
    u_mat = u_ref[...]
    row = lax.broadcasted_iota(jnp.int32, (rows, PAGE_SIZE), 0)
    col = lax.broadcasted_iota(jnp.int32, (rows, PAGE_SIZE), 1)

    @pl.when(p == 0)
    def _():
        for n in seqs:
            pv, carry = _sb_block(q_ref[n], knew_ref[n].astype(BF16), vnew_ref[n].astype(BF16), u_mat, bias,
                                  jnp.zeros((rows, 1), F32), col < lax.rem(row, t_new))
            acc_ref[n] = pv
            carry_ref[n] = jnp.broadcast_to(carry, (rows, PAGE_SIZE))

    @pl.when(p > 0)
    def _():
        carry = [carry_ref[n, :, 0:1] for n in seqs]
        acc = [acc_ref[n] for n in seqs]
        for g in range(group):
            for n in seqs:
                pv, carry[n] = _sb_block(q_ref[n], _page_rows(k_refs[n * group + g]),
                                         _page_rows(v_refs[n * group + g]), u_mat, bias, carry[n], None)
                acc[n] = acc[n] + pv
        for n in seqs:
            acc_ref[n] = acc[n]
            carry_ref[n] = jnp.broadcast_to(carry[n], (rows, PAGE_SIZE))

    @pl.when(p == n_steps)
    def _():
        for n in seqs:
            o = jnp.zeros((rows, HEAD_DIM), F32)
            for h in range(N_HEADS):
                in_head = (row >= h * t_new) & (row < (h + 1) * t_new)
                o = o + jnp.where(in_head, acc_ref[n, :, h * HEAD_DIM:(h + 1) * HEAD_DIM], 0.0)
            o_ref[n] = (_rms_scale(o) * nw_ref[...]).astype(o_ref.dtype)


def _sb_sample(page_table, q_rows, bias_rows, k_new, v_new, cache_k, cache_v, layer, u_mat, norm_rows):
    b, rows, _ = q_rows.shape
    n_pages = page_table.shape[1]
    t_new = rows // N_HEADS
    group = max(g for g in range(1, SAMPLE_PAGES_PER_STEP + 1) if n_pages % g == 0)
    n_steps = n_pages // group
    n_seq = SAMPLE_SEQS_PER_STEP if b % SAMPLE_SEQS_PER_STEP == 0 else 1

    def page_spec(n, g):
        def page_map(bi, p, pt):
            return (layer, pt[bi * n_seq + n, n_pages - 1 - (jnp.maximum(p - 1, 0) * group + g)], 0, 0)
        return pl.BlockSpec((None, None, PAGE_SIZE * N_HEADS, HEAD_DIM), page_map)

    page_specs = [page_spec(n, g) for n in range(n_seq) for g in range(group)]
    grid_spec = pltpu.PrefetchScalarGridSpec(
        num_scalar_prefetch=1,
        grid=(b // n_seq, n_steps + 1),
        in_specs=[pl.BlockSpec((n_seq, rows, GROUP_WIDTH), lambda bi, p, pt: (bi, 0, 0)),
                  pl.BlockSpec((rows, PAGE_SIZE), lambda bi, p, pt: (0, 0)),
                  pl.BlockSpec((n_seq, PAGE_SIZE, GROUP_WIDTH), lambda bi, p, pt: (bi, 0, 0)),
                  pl.BlockSpec((n_seq, PAGE_SIZE, GROUP_WIDTH), lambda bi, p, pt: (bi, 0, 0))]
                 + page_specs + page_specs
                 + [pl.BlockSpec((PAGE_SIZE, PAGE_SIZE), lambda bi, p, pt: (0, 0)),
                    pl.BlockSpec((rows, HEAD_DIM), lambda bi, p, pt: (0, 0))],
        out_specs=pl.BlockSpec((n_seq, rows, HEAD_DIM), lambda bi, p, pt: (bi, 0, 0)),
        scratch_shapes=[pltpu.VMEM((n_seq, rows, GROUP_WIDTH), F32), pltpu.VMEM((n_seq, rows, PAGE_SIZE), F32)],
    )
    n_pg = n_seq * group
    return pl.pallas_call(
        functools.partial(_sb_sample_kernel, n_steps=n_steps, t_new=t_new, group=group, n_seq=n_seq),
        out_shape=jax.ShapeDtypeStruct((b, rows, HEAD_DIM), BF16),
        grid_spec=grid_spec,
        compiler_params=_cparams("parallel", "arbitrary"),
        name="sb_sample",
    )(page_table, q_rows, bias_rows, k_new, v_new, *([cache_k] * n_pg), *([cache_v] * n_pg), u_mat, norm_rows)


def _gdn_kernel(xq_ref, xk_ref, xv_ref, z_ref, ab_ref, c0q_ref, c0k_ref, c0v_ref, cwq_ref, cwk_ref, cwv_ref,
                alog_ref, dtb_ref, nw_ref, s0_ref, ltri_ref, lvl_ref, ob_ref, sfin_ref,
                s_scr, bq_scr, bk_scr, bv_scr, *, t_valid, n_chunks, n_seq):
    c = pl.program_id(1)
    ch = GDN_CHUNK
    seqs = range(n_seq)
    heads = range(n_seq * N_HEADS)
    quads = range(n_seq * N_HEADS // QUAD)

    @pl.when(c == 0)
    def _():
        for n in seqs:
            for h in range(N_HEADS):
                s_scr[n * N_HEADS + h] = s0_ref[n, h]
            bq_scr[n, 0:HALO, :] = c0q_ref[n]
            bk_scr[n, 0:HALO, :] = c0k_ref[n]
            bv_scr[n, 0:HALO, :] = c0v_ref[n]

    def conv_silu(x_ref, buf, cw_ref, n):
        buf[n, HALO:HALO + ch, :] = x_ref[n]
        x = buf[n]
        y = pltpu.roll(x, CONV_TAPS - 1, axis=0)[HALO:, :] * cw_ref[0:1, :]
        for tap in range(1, CONV_TAPS - 1):
            y = y + pltpu.roll(x, CONV_TAPS - 1 - tap, axis=0)[HALO:, :] * cw_ref[tap:tap + 1, :]
        y = y + x[HALO:, :] * cw_ref[CONV_TAPS - 1:CONV_TAPS, :]
        buf[n, 0:HALO, :] = x[ch:ch + HALO, :]
        return y * _sigmoid(y)

    yq = jnp.concatenate([conv_silu(xq_ref, bq_scr, cwq_ref, n) for n in seqs], axis=1)
    yk = jnp.concatenate([conv_silu(xk_ref, bk_scr, cwk_ref, n) for n in seqs], axis=1)
    yv = jnp.concatenate([conv_silu(xv_ref, bv_scr, cwv_ref, n) for n in seqs], axis=1)

    ltri = ltri_ref[...]
    tok = c * ch + lax.broadcasted_iota(jnp.int32, (ch, HEAD_DIM), 0)
    valid = tok < t_valid
    beta, gam_c, e_gam, e_rest, g_tot = [], [], [], [], []
    for n in seqs:
        ab = ab_ref[n]
        log_g = jnp.where(valid, -jnp.exp(alog_ref[...]) * _softplus(ab + dtb_ref[...]), 0.0)
        beta.append(jnp.where(valid, _sigmoid(ab), 0.0))
        g_hi, g_mid, g_lo = _split3(log_g)
        gam = _dot(ltri, g_hi) + (_dot(ltri, g_mid) + _dot(ltri, g_lo))
        gam_last = gam[ch - 1:ch, :]
        gam_c.append(gam)
        e_gam.append(jnp.exp(gam))
        e_rest.append(jnp.exp(gam_last - gam))
        g_tot.append(jnp.exp(gam_last))

    hsl = [slice(h * HEAD_DIM, (h + 1) * HEAD_DIM) for h in heads]
    qsl = [slice(g * QUAD * HEAD_DIM, (g + 1) * QUAD * HEAD_DIM) for g in quads]
    lsl = [slice((h % QUAD) * HEAD_DIM, (h % QUAD + 1) * HEAD_DIM) for h in heads]

    def head_col(mats, h, first_col=0):
        col = first_col + h % N_HEADS
        return mats[h // N_HEADS][:, col:col + 1]

    def per_head(mats, first_col):
        return jnp.concatenate([jnp.broadcast_to(head_col(mats, h, first_col), (ch, HEAD_DIM)) for h in heads], axis=1)

    def l2_normalize(y):
        return jnp.concatenate([y[:, sl] * lax.rsqrt(jnp.sum(y[:, sl] * y[:, sl], axis=-1, keepdims=True) + NORM_EPS)
                                for sl in hsl], axis=1)

    beta_x = per_head(beta, N_HEADS)
    e_gam_x = per_head(e_gam, 0)
    q = l2_normalize(yq) * HEAD_DIM ** -0.5
    k = l2_normalize(yk)
    k_beta = k * beta_x
    v_beta = yv * beta_x
    k_beta_gam = k_beta * e_gam_x
    q_gam = (q * e_gam_x).astype(BF16)
    k_rest = (k * per_head(e_rest, 0)).astype(BF16)

    qw = QUAD * ch
    ri = lax.broadcasted_iota(jnp.int32, (ch, qw), 0)
    li = lax.broadcasted_iota(jnp.int32, (ch, qw), 1)
    ci = li & (ch - 1)
    incl = ri >= ci
    strict = ri > ci
    eye = jnp.where(ri == ci, 1.0, 0.0).astype(F32)
    masks_mat = _lane_block_masks(ch, QUAD, ch)
    masks_vec = _lane_block_masks(ch, QUAD, HEAD_DIM)

    a_mat, attn, inv = [], [], []
    for g in quads:
        gam_q = jnp.broadcast_to(head_col(gam_c, QUAD * g + QUAD - 1), (ch, qw))
        for j in range(QUAD - 2, -1, -1):
            gam_q = jnp.where(li < (j + 1) * ch, jnp.broadcast_to(head_col(gam_c, QUAD * g + j), (ch, qw)), gam_q)
        diff = gam_q - jnp.sum(gam_q * eye, axis=0, keepdims=True)
        decay = jnp.where(incl, jnp.exp(jnp.where(incl, diff, 0.0)), 0.0)
        k_diag = _lane_block_diag(k[:, qsl[g]].astype(BF16), masks_vec)
        a_mat.append(jnp.where(strict, _dot_nt(k_beta[:, qsl[g]].astype(BF16), k_diag) * decay, 0.0))
        attn.append((_dot_nt(q[:, qsl[g]].astype(BF16), k_diag) * decay).astype(BF16))
        inv.append(eye - a_mat[g] * lvl_ref[0])
    for lvl in range(1, INV_LEVELS):
        inner = [_mm3_block_diag(a_mat[g] * lvl_ref[lvl], inv[g], masks_mat) for g in quads]
        inv = [inv[g] - _mm3_block_diag(inv[g], inner[g], masks_mat) for g in quads]
    u = [_mm3_block_diag(inv[g], v_beta[:, qsl[g]], masks_vec) for g in quads]
    w = [_mm3_block_diag(inv[g], k_beta_gam[:, qsl[g]], masks_vec) for g in quads]
    s_old = [s_scr[h] for h in heads]
    s_bf = [s.astype(BF16) for s in s_old]
    v_new = [u[h // QUAD][:, lsl[h]] - _dot(w[h // QUAD][:, lsl[h]].astype(BF16), s_bf[h]) for h in heads]
    o_state = [_dot(q_gam[:, hsl[h]], s_bf[h]) for h in heads]
    v_new_bf = [v.astype(BF16) for v in v_new]
    for h in heads:
        s_scr[h] = s_old[h] * head_col(g_tot, h) + _dot_tn(k_rest[:, hsl[h]], v_new_bf[h])
    o_chunk = [_dot(attn[g], _lane_block_diag(jnp.concatenate(v_new_bf[QUAD * g:QUAD * (g + 1)], axis=1), masks_vec))
               for g in quads]
    for h in heads:
        o = o_state[h] + o_chunk[h // QUAD][:, lsl[h]]
        n, sl = h // N_HEADS, hsl[h % N_HEADS]
        z = z_ref[n, :, sl]
        ob_ref[n, :, sl] = (_rms_scale(o) * nw_ref[...] * (z * _sigmoid(z))).astype(ob_ref.dtype)

    @pl.when(c == n_chunks - 1)
    def _():
        for n in seqs:
            for h in range(N_HEADS):
                sfin_ref[n, h] = s_scr[n * N_HEADS + h]


def _gdn(proj3, ab3, conv0, conv_w, alog_row, dtb_row, norm_w, s0, ltri, lvl_masks, t_valid):
    b, t, _ = proj3.shape
    ch = GDN_CHUNK
    n_chunks = t // ch
    gw = GROUP_WIDTH
    n_seq = GDN_SEQS_PER_STEP if b % GDN_SEQS_PER_STEP == 0 else 1

    def col_spec(rows, blk_col):
        return pl.BlockSpec((n_seq, rows, gw), lambda bi, c: (bi, c if rows == ch else 0, blk_col))

    def cw_spec(blk_col):
        return pl.BlockSpec((HALO, gw), lambda bi, c: (0, blk_col))

    row_spec = pl.BlockSpec((1, HEAD_DIM), lambda bi, c: (0, 0))
    state_spec = pl.BlockSpec((n_seq, N_HEADS, HEAD_DIM, HEAD_DIM), lambda bi, c: (bi, 0, 0, 0))
    return pl.pallas_call(
        functools.partial(_gdn_kernel, t_valid=t_valid, n_chunks=n_chunks, n_seq=n_seq),
        out_shape=(jax.ShapeDtypeStruct((b, t, gw), BF16),
                   jax.ShapeDtypeStruct((b, N_HEADS, HEAD_DIM, HEAD_DIM), F32)),
        grid=(b // n_seq, n_chunks),
        in_specs=[col_spec(ch, 3), col_spec(ch, 4), col_spec(ch, 5), col_spec(ch, 6),
                  pl.BlockSpec((n_seq, ch, HEAD_DIM), lambda bi, c: (bi, c, 0)),
                  col_spec(HALO, 0), col_spec(HALO, 1), col_spec(HALO, 2),
                  cw_spec(0), cw_spec(1), cw_spec(2),
                  row_spec, row_spec, row_spec, state_spec,
                  pl.BlockSpec((ch, ch), lambda bi, c: (0, 0)),
                  pl.BlockSpec((INV_LEVELS, ch, QUAD * ch), lambda bi, c: (0, 0, 0))],
        out_specs=(pl.BlockSpec((n_seq, ch, gw), lambda bi, c: (bi, c, 0)), state_spec),
        scratch_shapes=[pltpu.VMEM((n_seq * N_HEADS, HEAD_DIM, HEAD_DIM), F32),
                        pltpu.VMEM((n_seq, HALO + ch, gw), F32),
                        pltpu.VMEM((n_seq, HALO + ch, gw), F32),
                        pltpu.VMEM((n_seq, HALO + ch, gw), F32)],
        compiler_params=_cparams("parallel", "arbitrary"),
        name="gdn",
    )(proj3, proj3, proj3, proj3, ab3, conv0, conv0, conv0, conv_w, conv_w, conv_w,
      alog_row, dtb_row, norm_w, s0, ltri, lvl_masks)


def _outproj_kernel(h_ref, oa_ref, ob_ref, wa_ref, wb_ref, o_ref):
    o_ref[...] = h_ref[...] + (_dot(oa_ref[...], wa_ref[...]) + _dot(ob_ref[...], wb_ref[...]))


def _outproj(h, oa, ob, w_out, layer, bm):
    m, d = h.shape
    gw = GROUP_WIDTH
    return pl.pallas_call(
        _outproj_kernel,
        out_shape=jax.ShapeDtypeStruct((m, d), F32),
        grid=(m // bm,),
        in_specs=[pl.BlockSpec((bm, d), lambda i: (i, 0)),
                  pl.BlockSpec((bm, gw), lambda i: (i, 0)),
                  pl.BlockSpec((bm, gw), lambda i: (i, 0)),
                  pl.BlockSpec((None, gw, d), lambda i: (layer, 0, 0)),
                  pl.BlockSpec((None, gw, d), lambda i: (layer, 1, 0))],
        out_specs=pl.BlockSpec((bm, d), lambda i: (i, 0)),
        compiler_params=_cparams("parallel"),
        name="outproj",
    )(h, oa, ob, w_out, w_out)


def _ffn_kernel(h_ref, nw_ref, wg_ref, wu_ref, wd_ref, fnw_ref, o_ref, xn_ref, acc_ref, *, n_f, final_norm):
    f = pl.program_id(1)

    @pl.when(f == 0)
    def _():
        xn_ref[...] = (_rms_scale(h_ref[...]) * nw_ref[...]).astype(BF16)
        acc_ref[...] = jnp.zeros_like(acc_ref)

    xn = xn_ref[...]
    g = _dot(xn, wg_ref[...])
    act = (g * _sigmoid(g)) * _dot(xn, wu_ref[...])
    acc_ref[...] += _dot(act.astype(BF16), wd_ref[...])

    @pl.when(f == n_f - 1)
    def _():
        out = h_ref[...] + acc_ref[...]
        if final_norm:
            out = _rms_scale(out) * fnw_ref[...]
        o_ref[...] = out


def _ffn(h, norm_w, w_gate, w_up, w_down, final_w, layer, bm, bf, final_norm):
    m, d = h.shape
    f_dim = w_gate.shape[2]
    n_f = f_dim // bf
    return pl.pallas_call(
        functools.partial(_ffn_kernel, n_f=n_f, final_norm=final_norm),
        out_shape=jax.ShapeDtypeStruct((m, d), F32),
        grid=(m // bm, n_f),
        in_specs=[pl.BlockSpec((bm, d), lambda i, f: (i, 0)),
                  pl.BlockSpec((1, d), lambda i, f: (0, 0)),
                  pl.BlockSpec((None, d, bf), lambda i, f: (layer, 0, f)),
                  pl.BlockSpec((None, d, bf), lambda i, f: (layer, 0, f)),
                  pl.BlockSpec((None, bf, d), lambda i, f: (layer, f, 0)),
                  pl.BlockSpec((1, d), lambda i, f: (0, 0))],
        out_specs=pl.BlockSpec((bm, d), lambda i, f: (i, 0)),
        scratch_shapes=[pltpu.VMEM((bm, d), BF16), pltpu.VMEM((bm, d), F32)],
        compiler_params=_cparams("parallel", "arbitrary"),
        name="ffn",
    )(h, norm_w, w_gate, w_up, w_down, final_w)


def _row_block(m, target):
    return target if m % target == 0 else m


def _excl_upper(n):
    j = lax.broadcasted_iota(jnp.int32, (n, n), 0)
    s = lax.broadcasted_iota(jnp.int32, (n, n), 1)
    return (j > s).astype(BF16)


def _pad_lanes(row, width):
    return jnp.pad(row, (0, width - row.shape[0])).reshape(1, width)


def kernel(x_prompt, x_sample, cache_k, cache_v, state_S, state_conv, page_table, norm_mix_w, w_in, conv_w, a_log, dt_bias, sb_bias, sb_norm_w, gdn_norm_w, w_out, norm_ffn_w, w_gate, w_up, w_down, final_norm_w):
    bp, tp, d = x_prompt.shape
    bs, ts, _ = x_sample.shape
    depth = w_in.shape[0]
    gw = GROUP_WIDTH
    main_cols = N_MAIN_BLOCKS * gw
    qkv_cols = 3 * gw
    ch = GDN_CHUNK
    assert w_in.shape[2] == main_cols + 2 * N_HEADS and tp % (2 * SB_BLOCK) == 0 and ts <= ch
    assert cache_k.shape[2:] == (PAGE_SIZE, N_HEADS, HEAD_DIM)

    w_in_bf = w_in.astype(BF16)
    w_ab_bf = jnp.pad(w_in[:, :, main_cols:].astype(BF16), ((0, 0), (0, 0), (0, HEAD_DIM - 2 * N_HEADS)))
    w_out_bf = w_out.astype(BF16)
    w_gate_bf = w_gate.astype(BF16)
    w_up_bf = w_up.astype(BF16)
    w_down_bf = w_down.astype(BF16)
    conv_w8 = jnp.pad(conv_w, ((0, 0), (0, HALO - CONV_TAPS), (0, 0)))
    cache_k = cache_k.reshape(cache_k.shape[:2] + (PAGE_SIZE * N_HEADS, HEAD_DIM))
    cache_v = cache_v.reshape(cache_v.shape[:2] + (PAGE_SIZE * N_HEADS, HEAD_DIM))

    u_prompt = _excl_upper(SB_BLOCK)
    u_page = _excl_upper(PAGE_SIZE)
    ri = lax.broadcasted_iota(jnp.int32, (ch, ch), 0)
    ci = lax.broadcasted_iota(jnp.int32, (ch, ch), 1)
    ltri = (ri >= ci).astype(BF16)
    lvl_masks = jnp.stack([((ri // (2 * b) == ci // (2 * b)) & ((ri & b) != 0) & ((ci & b) == 0)).astype(F32)
                           for b in (1 << e for e in range(INV_LEVELS))])
    lvl_masks = jnp.tile(lvl_masks, (1, 1, QUAD))
    head_eye = jnp.eye(N_HEADS, dtype=F32)
    final_w = final_norm_w.reshape(1, d)

    hp = x_prompt.reshape(bp * tp, d)
    hs = x_sample.reshape(bs * ts, d)
    bm_p = _row_block(bp * tp, DENSE_ROW_BLOCK)
    bm_s = bs * ts
    conv0_p = jnp.zeros((bp, HALO, qkv_cols), F32)
    s0_p = jnp.zeros((bp, N_HEADS, HEAD_DIM, HEAD_DIM), F32)

    outs = [[] for _ in range(8)]
    kp_rows = vp_rows = ks_rows = vs_rows = None
    for l in range(depth):
        nmw = norm_mix_w[l].reshape(1, d)
        nfw = norm_ffn_w[l].reshape(1, d)
        alog_row = _pad_lanes(a_log[l], HEAD_DIM)
        dtb_row = _pad_lanes(dt_bias[l], HEAD_DIM)
        gnw = gdn_norm_w[l].reshape(1, HEAD_DIM)
        last = l == depth - 1

        proj, ab, kp_rows, vp_rows = _inproj(hp, nmw, w_in_bf, w_ab_bf, bm_p, l, depth, kp_rows, vp_rows)
        proj3 = proj.reshape(bp, tp, main_cols)
        oa = _sb_prompt(proj3, sb_bias[l], u_prompt, sb_norm_w[l])
        ob, s_new = _gdn(proj3, ab.reshape(bp, tp, HEAD_DIM), conv0_p, conv_w8[l], alog_row, dtb_row, gnw,
                         s0_p, ltri, lvl_masks, tp)
        hp = _outproj(hp, oa.reshape(bp * tp, gw), ob.reshape(bp * tp, gw), w_out_bf, l, bm_p)
        hp = _ffn(hp, nfw, w_gate_bf, w_up_bf, w_down_bf, final_w, l, bm_p, FFN_COL_BLOCK, last)
        outs[4].append(s_new)
        outs[6].append(proj3[:, tp - (CONV_TAPS - 1):, 3 * gw:6 * gw])

        proj, ab, ks_rows, vs_rows = _inproj(hs, nmw, w_in_bf, w_ab_bf, bm_s, l, depth, ks_rows, vs_rows)
        proj3 = proj.reshape(bs, ts, main_cols)
        k_new = proj3[:, :, gw:2 * gw]
        v_new = proj3[:, :, 2 * gw:3 * gw]
        q4 = proj3[:, :, :gw].reshape(bs, ts, N_HEADS, HEAD_DIM) * SB_QUERY_SCALE
        q_rows = jnp.einsum('bthd,hg->bhtgd', q4, head_eye).reshape(bs, N_HEADS * ts, gw).astype(BF16)
        bias_rows = jnp.broadcast_to(jnp.repeat(sb_bias[l] * -LOG2_E, ts)[:, None], (N_HEADS * ts, PAGE_SIZE))
        norm_rows = jnp.repeat(sb_norm_w[l], ts, axis=0)
        pad_keys = ((0, 0), (0, PAGE_SIZE - ts), (0, 0))
        oa = _sb_sample(page_table, q_rows, bias_rows, jnp.pad(k_new, pad_keys), jnp.pad(v_new, pad_keys),
                        cache_k, cache_v, l, u_page, norm_rows)
        oa = oa.reshape(bs, N_HEADS, ts, HEAD_DIM).transpose(0, 2, 1, 3).reshape(bs * ts, gw)
        pad_chunk = ((0, 0), (0, ch - ts), (0, 0))
        conv0_s = jnp.pad(state_conv[l], ((0, 0), (HALO - (CONV_TAPS - 1), 0), (0, 0)))
        ob, s_new = _gdn(jnp.pad(proj3, pad_chunk), jnp.pad(ab.reshape(bs, ts, HEAD_DIM), pad_chunk), conv0_s,
                         conv_w8[l], alog_row, dtb_row, gnw, state_S[l], ltri, lvl_masks, ts)
        hs = _outproj(hs, oa, ob[:, :ts].reshape(bs * ts, gw), w_out_bf, l, bm_s)
        hs = _ffn(hs, nfw, w_gate_bf, w_up_bf, w_down_bf, final_w, l, bm_s, FFN_COL_BLOCK, last)
        outs[5].append(s_new)
        xc_tail = jnp.concatenate([state_conv[l], proj3[:, :, 3 * gw:6 * gw]], axis=1)
        outs[7].append(xc_tail[:, xc_tail.shape[1] - (CONV_TAPS - 1):])

    outs[0] = kp_rows.reshape(depth, bp, tp, N_HEADS, HEAD_DIM)
    outs[1] = vp_rows.reshape(depth, bp, tp, N_HEADS, HEAD_DIM)
    outs[2] = ks_rows.reshape(depth, bs, ts, N_HEADS, HEAD_DIM)
    outs[3] = vs_rows.reshape(depth, bs, ts, N_HEADS, HEAD_DIM)
    return (hp.reshape(bp, tp, d), hs.reshape(bs, ts, d)) + tuple(o if i < 4 else jnp.stack(o)
                                                                  for i, o in enumerate(outs))
```
